```python
import jax, jax.numpy as jnp
from jax import lax
import numpy as np

D_MODEL = 1024
BATCH = 1
SEQ = 16384
DEPTH = 2
DEC_BATCH = 32
DEC_SEQ = 1
PAST_LEN = 16384
PAGE_SIZE = 128

N_HEADS = 8
HEAD_DIM = 64
ATT_WIDTH = N_HEADS * HEAD_DIM
IDX_HEADS = 8
IDX_DIM = 64
TOP_K_MAX = 256
LRU_WIDTH = D_MODEL
LRU_BLOCKS = 16
LRU_BLOCK_DIM = LRU_WIDTH // LRU_BLOCKS
CONV_WIDTH = 4
LRU_C = 8.0
Q_BLOCK = 128
NORM_EPS = 1e-6
IDX_SCALE = (IDX_DIM ** -0.5) * (IDX_HEADS ** -0.5)
ATT_SCALE = HEAD_DIM ** -0.5
SPLITS = (ATT_WIDTH, ATT_WIDTH, ATT_WIDTH, ATT_WIDTH, IDX_HEADS * IDX_DIM, IDX_DIM, IDX_HEADS,
          LRU_WIDTH, LRU_WIDTH, D_MODEL, D_MODEL)
D_IN = sum(SPLITS)

kernel_name = "dsa_rglru_gated_hybrid_step"


def rmsnorm(x, g):
    xf = x.astype(jnp.float32)
    y = xf * lax.rsqrt(jnp.mean(xf * xf, axis=-1, keepdims=True) + NORM_EPS)
    return (y * g.astype(jnp.float32)).astype(x.dtype)


def project(h, w_in):
    B, T, _ = h.shape
    z = jnp.einsum('btd,de->bte', h, w_in)
    offs = [int(o) for o in np.cumsum(SPLITS)[:-1]]
    q, k, v, g_att, qi, ki, wi, xl, g_lru, ga, gb = jnp.split(z, offs, axis=-1)
    q = q.reshape(B, T, N_HEADS, HEAD_DIM)
    k = k.reshape(B, T, N_HEADS, HEAD_DIM)
    v = v.reshape(B, T, N_HEADS, HEAD_DIM)
    qi = qi.reshape(B, T, IDX_HEADS, IDX_DIM)
    return q, k, v, g_att, qi, ki, wi, xl, g_lru, ga, gb


def index_scores(qi, ki, wi):
    s = jax.nn.relu(jnp.einsum('bthd,bsd->bths', qi.astype(jnp.float32), ki.astype(jnp.float32)))
    return jnp.einsum('bth,bths->bts', wi.astype(jnp.float32), s) * IDX_SCALE


def sparse_attend(q, k_sel, v_sel, valid):
    logits = jnp.einsum('bthd,btkhd->bthk', q.astype(jnp.float32), k_sel.astype(jnp.float32)) * ATT_SCALE
    logits = jnp.where(valid[:, :, None, :], logits, -jnp.inf)
    p = jax.nn.softmax(logits, axis=-1)
    out = jnp.einsum('bthk,btkhd->bthd', p, v_sel.astype(jnp.float32))
    return out.astype(q.dtype)


def prompt_attention(q, k, v, qi, ki, wi):
    B, S = q.shape[:2]
    n_blk = S // Q_BLOCK
    k_top = min(TOP_K_MAX, S // 4)
    key_pos = jnp.arange(S)

    def to_blocks(a):
        return a.reshape((B, n_blk, Q_BLOCK) + a.shape[2:]).swapaxes(0, 1)

    def one_block(args):
        i, qb, qib, wib = args
        t_pos = i * Q_BLOCK + jnp.arange(Q_BLOCK)
        sc = index_scores(qib, ki, wib)
        sc = jnp.where((key_pos[None, :] <= t_pos[:, None])[None], sc, -jnp.inf)
        _, idx = lax.top_k(sc, k_top)
        valid = idx <= t_pos[None, :, None]
        k_sel = jax.vmap(lambda a, j: a[j])(k, idx)
        v_sel = jax.vmap(lambda a, j: a[j])(v, idx)
        return sparse_attend(qb, k_sel, v_sel, valid)

    out = lax.map(one_block, (jnp.arange(n_blk), to_blocks(q), to_blocks(qi), to_blocks(wi)))
    return out.swapaxes(0, 1).reshape(B, S, N_HEADS, HEAD_DIM)


def sample_attention(q, k_new, v_new, qi, ki_new, wi, ck, cv, cki, page_table):
    B, T = q.shape[:2]
    page = ck.shape[1]
    P = page_table.shape[1] * page
    L = P + T
    k_top = min(TOP_K_MAX, L // 4)
    ki_past = cki[page_table].reshape(B, P, IDX_DIM).astype(ki_new.dtype)
    ki_all = jnp.concatenate([ki_past, ki_new], axis=1)
    t_pos = P + jnp.arange(T)
    sc = index_scores(qi, ki_all, wi)
    sc = jnp.where((jnp.arange(L)[None, :] <= t_pos[:, None])[None], sc, -jnp.inf)
    _, idx = lax.top_k(sc, k_top)
    valid = idx <= t_pos[None, :, None]
    in_past = (idx < P)[..., None, None]
    bidx = jnp.arange(B)[:, None, None]
    pidx = jnp.minimum(idx, P - 1)
    phys = page_table[bidx, pidx // page]
    off = pidx % page
    nidx = jnp.clip(idx - P, 0, T - 1)
    k_sel = jnp.where(in_past, ck[phys, off].astype(k_new.dtype), k_new[bidx, nidx])
    v_sel = jnp.where(in_past, cv[phys, off].astype(v_new.dtype), v_new[bidx, nidx])
    return sparse_attend(q, k_sel, v_sel, valid)


def rglru_branch(x, conv_state, h0, conv_w, conv_b, w_rg, b_rg, w_ig, b_ig, lam):
    B, T, W = x.shape
    xpad = jnp.concatenate([conv_state.astype(x.dtype), x], axis=1)
    xc = conv_b + sum(conv_w[j] * xpad[:, j:j + T] for j in range(CONV_WIDTH))
    new_conv = xpad[:, T:]
    xf = xc.astype(jnp.float32)
    xb = xf.reshape(B, T, LRU_BLOCKS, LRU_BLOCK_DIM)
    r = jax.nn.sigmoid(jnp.einsum('btnd,nde->btne', xb, w_rg.astype(jnp.float32)).reshape(B, T, W)
                       + b_rg.astype(jnp.float32))
    i = jax.nn.sigmoid(jnp.einsum('btnd,nde->btne', xb, w_ig.astype(jnp.float32)).reshape(B, T, W)
                       + b_ig.astype(jnp.float32))
    log_a = -LRU_C * r * jax.nn.softplus(-lam.astype(jnp.float32))
    a = jnp.exp(log_a)
    u = jnp.sqrt(-jnp.expm1(2.0 * log_a)) * (i * xf)
    u = u.at[:, 0].add(a[:, 0] * h0.astype(jnp.float32))

    def combine(left, right):
        a1, b1 = left
        a2, b2 = right
        return a1 * a2, a2 * b1 + b2

    _, h = lax.associative_scan(combine, (a, u), axis=1)
    return h.astype(x.dtype), new_conv, h[:, -1].astype(x.dtype)


def merge(y_att, g_att, y_lru, g_lru, ga, gb, w_oa, w_ob, w_out):
    B, T = y_att.shape[:2]
    pa = jnp.einsum('bte,ed->btd', y_att.reshape(B, T, ATT_WIDTH) * jax.nn.silu(g_att), w_oa)
    pb = jnp.einsum('bte,ed->btd', y_lru * jax.nn.silu(g_lru), w_ob)
    m = jax.nn.sigmoid(ga) * pa + jax.nn.sigmoid(gb) * pb
    return jnp.einsum('btd,de->bte', m, w_out)


def setup_inputs(seed: int = 0) -> dict:
    key = jax.random.key(seed)
    ks = jax.random.split(key, 24)
    f32 = jnp.float32
    n_pages = PAST_LEN // PAGE_SIZE
    n_used = DEC_BATCH * n_pages
    n_pool = n_used + n_used // 4
    nrm = lambda k, s: jax.random.normal(k, s, f32)
    u = jax.random.uniform(ks[16], (DEPTH, LRU_WIDTH), f32, minval=0.9, maxval=0.999)
    s = u ** (1.0 / LRU_C)
    lru_lam = jnp.log(s) - jnp.log1p(-s)
    page_table = jax.random.permutation(ks[7], n_pool)[:n_used].reshape(DEC_BATCH, n_pages).astype(jnp.int32)
    return {
        "x_prompt": nrm(ks[0], (BATCH, SEQ, D_MODEL)),
        "x_sample": nrm(ks[1], (DEC_BATCH, DEC_SEQ, D_MODEL)),
        "cache_k": nrm(ks[2], (DEPTH, n_pool, PAGE_SIZE, N_HEADS, HEAD_DIM)),
        "cache_v": nrm(ks[3], (DEPTH, n_pool, PAGE_SIZE, N_HEADS, HEAD_DIM)),
        "cache_kidx": nrm(ks[4], (DEPTH, n_pool, PAGE_SIZE, IDX_DIM)),
        "state_conv": nrm(ks[5], (DEPTH, DEC_BATCH, CONV_WIDTH - 1, LRU_WIDTH)),
        "state_h": 0.5 * nrm(ks[6], (DEPTH, DEC_BATCH, LRU_WIDTH)),
        "page_table": page_table,
        "norm_g": 1.0 + 0.05 * nrm(ks[8], (DEPTH, D_MODEL)),
        "w_in": nrm(ks[9], (DEPTH, D_MODEL, D_IN)) * D_MODEL ** -0.5,
        "conv_w": nrm(ks[10], (DEPTH, CONV_WIDTH, LRU_WIDTH)) * CONV_WIDTH ** -0.5,
        "conv_b": 0.01 * nrm(ks[11], (DEPTH, LRU_WIDTH)),
        "w_rg": nrm(ks[12], (DEPTH, LRU_BLOCKS, LRU_BLOCK_DIM, LRU_BLOCK_DIM)) * LRU_BLOCK_DIM ** -0.5,
        "b_rg": 0.01 * nrm(ks[13], (DEPTH, LRU_WIDTH)),
        "w_ig": nrm(ks[14], (DEPTH, LRU_BLOCKS, LRU_BLOCK_DIM, LRU_BLOCK_DIM)) * LRU_BLOCK_DIM ** -0.5,
        "b_ig": 0.01 * nrm(ks[15], (DEPTH, LRU_WIDTH)),
        "lru_lam": lru_lam,
        "w_oa": nrm(ks[17], (DEPTH, ATT_WIDTH, D_MODEL)) * ATT_WIDTH ** -0.5,
        "w_ob": nrm(ks[18], (DEPTH, LRU_WIDTH, D_MODEL)) * LRU_WIDTH ** -0.5,
        "w_out": nrm(ks[19], (DEPTH, D_MODEL, D_MODEL)) * D_MODEL ** -0.5,
        "final_g": 1.0 + 0.05 * nrm(ks[20], (D_MODEL,)),
    }


def reference(x_prompt, x_sample, cache_k, cache_v, cache_kidx, state_conv, state_h, page_table,
              norm_g, w_in, conv_w, conv_b, w_rg, b_rg, w_ig, b_ig, lru_lam, w_oa, w_ob, w_out, final_g):
    xp, xs = x_prompt, x_sample
    Bp = xp.shape[0]
    pk, pv, pki, pconv, ph = [], [], [], [], []
    sk, sv, ski, sconv, sh = [], [], [], [], []
    for l in range(DEPTH):
        lru_p = (conv_w[l], conv_b[l], w_rg[l], b_rg[l], w_ig[l], b_ig[l], lru_lam[l])
        hp = rmsnorm(xp, norm_g[l])
        q, k, v, g_att, qi, ki, wi, xl, g_lru, ga, gb = project(hp, w_in[l])
        ya = prompt_attention(q, k, v, qi, ki, wi)
        conv0 = jnp.zeros((Bp, CONV_WIDTH - 1, LRU_WIDTH), xp.dtype)
        h0 = jnp.zeros((Bp, LRU_WIDTH), xp.dtype)
        yl, cst, hl = rglru_branch(xl, conv0, h0, *lru_p)
        xp = xp + merge(ya, g_att, yl, g_lru, ga, gb, w_oa[l], w_ob[l], w_out[l])
        pk.append(k); pv.append(v); pki.append(ki); pconv.append(cst); ph.append(hl)
        hs = rmsnorm(xs, norm_g[l])
        q, k, v, g_att, qi, ki, wi, xl, g_lru, ga, gb = project(hs, w_in[l])
        ya = sample_attention(q, k, v, qi, ki, wi, cache_k[l], cache_v[l], cache_kidx[l], page_table)
        yl, cst, hl = rglru_branch(xl, state_conv[l], state_h[l], *lru_p)
        xs = xs + merge(ya, g_att, yl, g_lru, ga, gb, w_oa[l], w_ob[l], w_out[l])
        sk.append(k); sv.append(v); ski.append(ki); sconv.append(cst); sh.append(hl)
    y_prompt = rmsnorm(xp, final_g)
    y_sample = rmsnorm(xs, final_g)
    return (y_prompt, y_sample,
            jnp.stack(pk), jnp.stack(pv), jnp.stack(pki), jnp.stack(pconv), jnp.stack(ph),
            jnp.stack(sk), jnp.stack(sv), jnp.stack(ski), jnp.stack(sconv), jnp.stack(sh))
```

```python
import functools
import math

import jax
import jax.numpy as jnp
from jax import lax
from jax.experimental import pallas as pl
from jax.experimental.pallas import tpu as pltpu

F32 = jnp.float32
BF16 = jnp.bfloat16
I32 = jnp.int32

N_HEADS = 8
HEAD_DIM = 64
ATT_WIDTH = N_HEADS * HEAD_DIM
IDX_HEADS = 8
IDX_DIM = 64
IDX_WIDTH = IDX_HEADS * IDX_DIM
TOP_K_MAX = 256
LRU_BLOCKS = 16
CONV_WIDTH = 4
LRU_C = 8.0
NORM_EPS = 1e-6
IDX_SCALE = (IDX_DIM ** -0.5) * (IDX_HEADS ** -0.5)
ATT_SCALE = HEAD_DIM ** -0.5
LOG2E = 1.4426950408889634
NEG_BIG = -1e30
LANES = 128
SUBLANES = 8
VMEM_LIMIT_BYTES = 56 * 1024 * 1024

_NT = (((1,), (1,)), ((), ()))


def _dot(a, b):
    return jnp.dot(a, b, preferred_element_type=F32)


def _dot_nt(a, b):
    return lax.dot_general(a, b, _NT, preferred_element_type=F32)


def _rmsnorm(x, g):
    return (x * lax.rsqrt(jnp.mean(x * x, axis=-1, keepdims=True) + NORM_EPS)) * g


def _sigmoid(x):
    return 1.0 / (1.0 + jnp.exp(-x))


def _silu(x):
    return x * _sigmoid(x)


def _log1p(e):
    w = 1.0 + e
    return jnp.where(w == 1.0, e, jnp.log(w) * (e / (w - 1.0)))


def _expm1(x):
    u = jnp.exp(x)
    um1 = u - 1.0
    return jnp.where(um1 == 0.0, x, jnp.where(u == 0.0, -1.0, um1 * (x / jnp.log(u))))


def _const_spec(shape):
    nd = len(shape)
    return pl.BlockSpec(shape, lambda *_: (0,) * nd, pipeline_mode=pl.Buffered(1))


def _params(sem):
    return pltpu.CompilerParams(dimension_semantics=sem, vmem_limit_bytes=VMEM_LIMIT_BYTES)


def _proj_kernel(x_ref, g_ref, wqkvg_ref, wkT_ref, wvT_ref, widx_ref, wkiT_ref,
                 qh_ref, qf_ref, k_ref, v_ref, kT_ref, vT_ref, gatt_ref, qi_ref, ki_ref, kiT_ref, wi_ref):
    hb = _rmsnorm(x_ref[...], g_ref[...]).astype(BF16)
    z = _dot(hb, wqkvg_ref[...])
    q = (z[:, 0:ATT_WIDTH] * (ATT_SCALE * LOG2E)).astype(BF16)
    qf_ref[...] = q
    for h in range(N_HEADS):
        qh_ref[h] = q[:, h * HEAD_DIM:(h + 1) * HEAD_DIM]
    k_ref[...] = z[:, ATT_WIDTH:2 * ATT_WIDTH]
    v_ref[...] = z[:, 2 * ATT_WIDTH:3 * ATT_WIDTH]
    gatt_ref[...] = z[:, 3 * ATT_WIDTH:4 * ATT_WIDTH]
    kT_ref[...] = _dot_nt(wkT_ref[...], hb).astype(BF16)
    vT_ref[...] = _dot_nt(wvT_ref[...], hb).astype(BF16)
    zi = _dot(hb, widx_ref[...])
    qi = zi[:, 0:IDX_WIDTH].astype(BF16)
    for h in range(IDX_HEADS):
        qi_ref[h] = qi[:, h * IDX_DIM:(h + 1) * IDX_DIM]
    ki_ref[...] = zi[:, IDX_WIDTH:IDX_WIDTH + IDX_DIM]
    wi_ref[...] = zi[:, IDX_WIDTH + IDX_DIM:IDX_WIDTH + IDX_DIM + IDX_HEADS]
    kiT_ref[...] = _dot_nt(wkiT_ref[...], hb).astype(BF16)


def _proj(x, g, w, tm):
    m, d = x.shape
    grid = (m // tm,)
    row = lambda c: pl.BlockSpec((tm, c), lambda i: (i, 0))
    col = lambda r: pl.BlockSpec((r, tm), lambda i: (0, i))
    heads = lambda n, c: pl.BlockSpec((n, tm, c), lambda i: (0, i, 0))
    out_shape = (
        jax.ShapeDtypeStruct((N_HEADS, m, HEAD_DIM), BF16),
        jax.ShapeDtypeStruct((m, ATT_WIDTH), BF16),
        jax.ShapeDtypeStruct((m, ATT_WIDTH), F32),
        jax.ShapeDtypeStruct((m, ATT_WIDTH), F32),
        jax.ShapeDtypeStruct((ATT_WIDTH, m), BF16),
        jax.ShapeDtypeStruct((ATT_WIDTH, m), BF16),
        jax.ShapeDtypeStruct((m, ATT_WIDTH), F32),
        jax.ShapeDtypeStruct((IDX_HEADS, m, IDX_DIM), BF16),
        jax.ShapeDtypeStruct((m, IDX_DIM), F32),
        jax.ShapeDtypeStruct((IDX_DIM, m), BF16),
        jax.ShapeDtypeStruct((m, IDX_HEADS), F32),
    )
    out_specs = (heads(N_HEADS, HEAD_DIM), row(ATT_WIDTH), row(ATT_WIDTH), row(ATT_WIDTH),
                 col(ATT_WIDTH), col(ATT_WIDTH), row(ATT_WIDTH), heads(IDX_HEADS, IDX_DIM),
                 row(IDX_DIM), col(IDX_DIM), row(IDX_HEADS))
    in_specs = [row(d), _const_spec((1, d)), _const_spec(w["qkvg"].shape), _const_spec(w["kT"].shape),
                _const_spec(w["vT"].shape), _const_spec(w["idx"].shape), _const_spec(w["kiT"].shape)]
    return pl.pallas_call(
        _proj_kernel, grid=grid, in_specs=in_specs, out_specs=out_specs, out_shape=out_shape,
        compiler_params=_params(("arbitrary",)), name="proj",
    )(x, g, w["qkvg"], w["kT"], w["vT"], w["idx"], w["kiT"])


MAX_BISECT = 30


def _select_topk(count_ge, min_ge, count_gt_min_gt, count_tie_le, amax, need, n_keys):
    shape = need.shape
    big_j = jnp.full(shape, n_keys, I32)
    hi0 = 2.0 * amax + 1e-30
    lo0 = -hi0

    def n_open(done):
        return jnp.sum(1.0 - done)

    def bis_cond(st):
        it, _, _, _, _, nopen = st
        return jnp.logical_and(it < MAX_BISECT, nopen > 0.0)

    def bis_body(st):
        it, lo, hi, thr, done, _ = st
        mid = 0.5 * (lo + hi)
        c = count_ge(mid)
        hit = jnp.logical_and(c == need, done == 0.0)
        thr = jnp.where(hit, mid, thr)
        done = jnp.where(hit, 1.0, done)
        lo = jnp.where(c > need, mid, lo)
        hi = jnp.where(c < need, mid, hi)
        return it + 1, lo, hi, thr, done, n_open(done)

    zeros = jnp.zeros(shape, F32)
    st = (jnp.int32(0), lo0, hi0, zeros, zeros, jnp.float32(1.0))
    _, lo, _, thr, done, _ = lax.while_loop(bis_cond, bis_body, st)

    def walk_cond(st):
        return st[-1] > 0.0

    def walk_body(st):
        lo, t_val, j_val, rem, done, tie, _ = st
        t_cand = min_ge(lo)
        c_gt, next_val = count_gt_min_gt(t_cand)
        fin = jnp.logical_and(c_gt <= need, done == 0.0)
        t_val = jnp.where(fin, t_cand, t_val)
        has_tie = jnp.logical_and(fin, c_gt < need)
        j_val = jnp.where(jnp.logical_and(fin, c_gt == need), -1, j_val)
        rem = jnp.where(has_tie, need - c_gt, rem)
        tie = jnp.where(has_tie, 1.0, tie)
        done = jnp.where(fin, 1.0, done)
        lo = jnp.where(done == 0.0, next_val, lo)
        return lo, t_val, j_val, rem, done, tie, n_open(done)

    st = (lo, thr, big_j, zeros, done, zeros, n_open(done))
    _, t_val, j_val, rem, _, tie, _ = lax.while_loop(walk_cond, walk_body, st)

    def tie_cond(st):
        it, _, _, ntie = st
        return jnp.logical_and(it < 32, ntie > 0.0)

    def tie_body(st):
        it, jlo, jhi, ntie = st
        jm = lax.shift_right_arithmetic(jlo + jhi, 1)
        c = count_tie_le(t_val, jm)
        ge = c >= rem
        jhi = jnp.where(ge, jm, jhi)
        jlo = jnp.where(ge, jlo, jm)
        return it + 1, jlo, jhi, ntie

    st = (jnp.int32(0), jnp.full(shape, -1, I32), jnp.full(shape, n_keys - 1, I32), jnp.sum(tie))
    _, _, jhi, _ = lax.while_loop(tie_cond, tie_body, st)
    j_val = jnp.where(tie > 0.0, jhi, j_val)
    return t_val, j_val


def _prompt_attn_kernel(qh_ref, qi_ref, wi_ref, kT_ref, vT_ref, kiT_ref, o_ref,
                        sc_ref, m_ref, l_ref, acc_ref, *, tq, ch, seq):
    i = pl.program_id(0)
    n_ch = ((i + 1) * tq + ch - 1) // ch
    n_sub = ch // LANES
    q_pos = i * tq + lax.broadcasted_iota(I32, (tq, 1), 0)
    lane_iota = lax.broadcasted_iota(I32, (tq, LANES), 1)

    w_idx = wi_ref[...] * IDX_SCALE
    w_b = [jnp.broadcast_to(w_idx[:, h:h + 1], (tq, LANES)) for h in range(IDX_HEADS)]

    def score_body(c, amax):
        start = pl.multiple_of(c * ch, ch)
        for j in range(n_sub):
            kic = kiT_ref[:, pl.ds(start + j * LANES, LANES)]
            acc = jnp.zeros((tq, LANES), F32)
            for h in range(IDX_HEADS):
                x = _dot(qi_ref[h], kic)
                acc = acc + w_b[h] * jnp.maximum(x, 0.0)
            amax = jnp.maximum(amax, jnp.abs(acc))
            key_pos = start + j * LANES + lane_iota
            sc_ref[:, pl.ds(start + j * LANES, LANES)] = jnp.where(key_pos <= q_pos, acc, -jnp.inf)
        return amax

    amax = lax.fori_loop(0, n_ch, score_body, jnp.zeros((tq, LANES), F32))
    amax = jnp.max(amax, axis=-1, keepdims=True)

    def row_reduce(fn, init, combine, final):
        def body(c, acc):
            start = pl.multiple_of(c * ch, ch)
            for j in range(n_sub):
                x = sc_ref[:, pl.ds(start + j * LANES, LANES)]
                acc = combine(acc, fn(x, start + j * LANES + lane_iota))
            return acc
        return final(lax.fori_loop(0, n_ch, body, jnp.full((tq, LANES), init, F32)))

    rsum = lambda a: jnp.sum(a, axis=-1, keepdims=True)
    rmin = lambda a: jnp.min(a, axis=-1, keepdims=True)
    bc = lambda v: jnp.broadcast_to(v, (tq, LANES))

    def count_ge(t):
        tb = bc(t)
        return row_reduce(lambda x, _: jnp.where(x >= tb, 1.0, 0.0), 0.0, jnp.add, rsum)

    def min_ge(t):
        tb = bc(t)
        return row_reduce(lambda x, _: jnp.where(x >= tb, x, jnp.inf), jnp.inf, jnp.minimum, rmin)

    def count_gt_min_gt(t):
        tb = bc(t)
        cnt = row_reduce(lambda x, _: jnp.where(x > tb, 1.0, 0.0), 0.0, jnp.add, rsum)
        nxt = row_reduce(lambda x, _: jnp.where(x > tb, x, jnp.inf), jnp.inf, jnp.minimum, rmin)
        return cnt, nxt

    def count_tie_le(t, jm):
        tb, jb = bc(t), bc(jm)
        return row_reduce(lambda x, kp: jnp.where(jnp.logical_and(x == tb, kp <= jb), 1.0, 0.0),
                          0.0, jnp.add, rsum)

    need = jnp.minimum(q_pos + 1, min(TOP_K_MAX, seq // 4)).astype(F32)
    t_val, j_val = _select_topk(count_ge, min_ge, count_gt_min_gt, count_tie_le, amax, need, seq)
    t_b, j_b = bc(t_val), bc(j_val)

    m_ref[...] = jnp.full(m_ref.shape, NEG_BIG, F32)
    l_ref[...] = jnp.zeros(l_ref.shape, F32)
    acc_ref[...] = jnp.zeros(acc_ref.shape, F32)

    def attn_body(c, carry):
        start = pl.multiple_of(c * ch, ch)
        bias = []
        for j in range(n_sub):
            x = sc_ref[:, pl.ds(start + j * LANES, LANES)]
            kp = start + j * LANES + lane_iota
            sel = jnp.logical_or(x > t_b, jnp.logical_and(x == t_b, kp <= j_b))
            bias.append(jnp.where(sel, 0.0, -jnp.inf))
        bias = jnp.concatenate(bias, axis=-1)
        for h in range(N_HEADS):
            rows = pl.ds(h * HEAD_DIM, HEAD_DIM)
            s = _dot(qh_ref[h], kT_ref[rows, pl.ds(start, ch)]) + bias
            m_prev = m_ref[h]
            m_new = jnp.maximum(m_prev, jnp.max(s, axis=-1, keepdims=True))
            p = jnp.exp2(s - m_new)
            alpha = jnp.exp2(m_prev - m_new)
            l_ref[h] = alpha * l_ref[h] + jnp.sum(p, axis=-1, keepdims=True)
            acc_ref[h] = alpha * acc_ref[h] + _dot_nt(p.astype(BF16), vT_ref[rows, pl.ds(start, ch)])
            m_ref[h] = m_new
        return carry

    lax.fori_loop(0, n_ch, attn_body, 0)
    for h in range(N_HEADS):
        o_ref[:, h * HEAD_DIM:(h + 1) * HEAD_DIM] = acc_ref[h] / l_ref[h]


def _prompt_attention(qh, qi, wi, kT, vT, kiT, tq, ch):
    seq = qh.shape[1]
    kern = functools.partial(_prompt_attn_kernel, tq=tq, ch=ch, seq=seq)
    heads = lambda n, c: pl.BlockSpec((n, tq, c), lambda i: (0, i, 0))
    return pl.pallas_call(
        kern, grid=(seq // tq,),
        in_specs=[heads(N_HEADS, HEAD_DIM), heads(IDX_HEADS, IDX_DIM),
                  pl.BlockSpec((tq, IDX_HEADS), lambda i: (i, 0)),
                  _const_spec(kT.shape), _const_spec(vT.shape), _const_spec(kiT.shape)],
        out_specs=pl.BlockSpec((tq, ATT_WIDTH), lambda i: (i, 0)),
        out_shape=jax.ShapeDtypeStruct((seq, ATT_WIDTH), F32),
        scratch_shapes=[pltpu.VMEM((tq, seq), F32),
                        pltpu.VMEM((N_HEADS, tq, 1), F32),
                        pltpu.VMEM((N_HEADS, tq, 1), F32),
                        pltpu.VMEM((N_HEADS, tq, HEAD_DIM), F32)],
        compiler_params=_params(("arbitrary",)), name="prompt_attention",
    )(qh, qi, wi, kT, vT, kiT)


def _lru_gates(xc, wrg_ref, brg_ref, wig_ref, big_ref, lam_ref):
    xcb = xc.astype(BF16)
    r = _sigmoid(_dot(xcb, wrg_ref[...]) + brg_ref[...])
    ig = _sigmoid(_dot(xcb, wig_ref[...]) + big_ref[...])
    nlam = -lam_ref[...]
    softplus = jnp.maximum(nlam, 0.0) + _log1p(jnp.exp(-jnp.abs(nlam)))
    log_a = (-LRU_C) * r * softplus
    a = jnp.exp(log_a)
    u = jnp.sqrt(-_expm1(2.0 * log_a)) * (ig * xc)
    return a, u


def _merge(x, ya, g_att, yl, g_lru, ga, gb, woa_ref, wob_ref, wout_ref):
    pa = _dot((ya * _silu(g_att)).astype(BF16), woa_ref[...])
    pb = _dot((yl * _silu(g_lru)).astype(BF16), wob_ref[...])
    m = _sigmoid(ga) * pa + _sigmoid(gb) * pb
    return x + _dot(m.astype(BF16), wout_ref[...])


def _prompt_lru_kernel(x_ref, g_ref, wlru_ref, cw_ref, cb_ref, wrg_ref, brg_ref, wig_ref, big_ref, lam_ref,
                       ya_ref, gatt_ref, woa_ref, wob_ref, wout_ref,
                       xo_ref, tail_ref, hlast_ref, xl_sc, h_sc, *, tm, width):
    i = pl.program_id(0)

    @pl.when(i == 0)
    def _():
        xl_sc[0:SUBLANES, :] = jnp.zeros((SUBLANES, width), F32)
        h_sc[...] = jnp.zeros(h_sc.shape, F32)

    x = x_ref[...]
    hb = _rmsnorm(x, g_ref[...]).astype(BF16)
    z = _dot(hb, wlru_ref[...])
    xl_sc[SUBLANES:SUBLANES + tm, :] = z[:, 0:width]
    xc = cb_ref[...]
    for j in range(CONV_WIDTH):
        xc = xc + cw_ref[j:j + 1, :] * xl_sc[pl.ds(SUBLANES - (CONV_WIDTH - 1) + j, tm), :]
    a, b = _lru_gates(xc, wrg_ref, brg_ref, wig_ref, big_ref, lam_ref)

    t_idx = lax.broadcasted_iota(I32, (tm, width), 0)
    d = 1
    while d < tm:
        a_s = pltpu.roll(a, d, 0)
        b_s = pltpu.roll(b, d, 0)
        live = t_idx >= d
        b = jnp.where(live, a * b_s + b, b)
        a = jnp.where(live, a * a_s, a)
        d *= 2
    hs = a * h_sc[0:1, :] + b
    h_sc[...] = jnp.broadcast_to(hs[tm - 1:tm, :], h_sc.shape)
    hlast_ref[...] = h_sc[...]
    tail_ref[...] = xl_sc[tm:tm + SUBLANES, :]
    xl_sc[0:SUBLANES, :] = xl_sc[tm:tm + SUBLANES, :]

    xo_ref[...] = _merge(x, ya_ref[...], gatt_ref[...], hs, z[:, width:2 * width],
                         z[:, 2 * width:3 * width], z[:, 3 * width:4 * width], woa_ref, wob_ref, wout_ref)


def _prompt_lru_merge(x, ya, gatt, w, tm):
    m, d = x.shape
    width = w["rg"].shape[0]
    kern = functools.partial(_prompt_lru_kernel, tm=tm, width=width)
    row = lambda c: pl.BlockSpec((tm, c), lambda i: (i, 0))
    consts = [w["g"], w["lru"], w["conv_w"], w["conv_b"], w["rg"], w["b_rg"], w["ig"], w["b_ig"], w["lam"]]
    consts2 = [w["oa"], w["ob"], w["out"]]
    return pl.pallas_call(
        kern, grid=(m // tm,),
        in_specs=[row(d)] + [_const_spec(c.shape) for c in consts] + [row(ATT_WIDTH), row(ATT_WIDTH)]
                 + [_const_spec(c.shape) for c in consts2],
        out_specs=(row(d), pl.BlockSpec((SUBLANES, width), lambda i: (0, 0)),
                   pl.BlockSpec((SUBLANES, width), lambda i: (0, 0))),
        out_shape=(jax.ShapeDtypeStruct((m, d), F32), jax.ShapeDtypeStruct((SUBLANES, width), F32),
                   jax.ShapeDtypeStruct((SUBLANES, width), F32)),
        scratch_shapes=[pltpu.VMEM((tm + SUBLANES, width), F32), pltpu.VMEM((SUBLANES, width), F32)],
        compiler_params=_params(("arbitrary",)), name="prompt_lru_merge",
    )(x, *consts, ya, gatt, *consts2)


def _sample_lru_kernel(x_ref, g_ref, wlru_ref, cw_ref, cb_ref, wrg_ref, brg_ref, wig_ref, big_ref, lam_ref,
                       c0_ref, c1_ref, c2_ref, h0_ref, ya_ref, gatt_ref, woa_ref, wob_ref, wout_ref,
                       xo_ref, xl_ref, h_ref, *, width):
    x = x_ref[...]
    hb = _rmsnorm(x, g_ref[...]).astype(BF16)
    z = _dot(hb, wlru_ref[...])
    xl = z[:, 0:width]
    xc = cb_ref[...]
    for j, past in enumerate((c0_ref, c1_ref, c2_ref)):
        xc = xc + cw_ref[j:j + 1, :] * past[...]
    xc = xc + cw_ref[CONV_WIDTH - 1:CONV_WIDTH, :] * xl
    a, u = _lru_gates(xc, wrg_ref, brg_ref, wig_ref, big_ref, lam_ref)
    h = u + a * h0_ref[...]
    xl_ref[...] = xl
    h_ref[...] = h
    xo_ref[...] = _merge(x, ya_ref[...], gatt_ref[...], h, z[:, width:2 * width],
                         z[:, 2 * width:3 * width], z[:, 3 * width:4 * width], woa_ref, wob_ref, wout_ref)


def _sample_lru_merge(x, conv_state, h0, ya, gatt, w):
    b, d = x.shape
    width = w["rg"].shape[0]
    kern = functools.partial(_sample_lru_kernel, width=width)
    args = [x, w["g"], w["lru"], w["conv_w"], w["conv_b"], w["rg"], w["b_rg"], w["ig"], w["b_ig"], w["lam"],
            conv_state[:, 0], conv_state[:, 1], conv_state[:, 2], h0, ya, gatt, w["oa"], w["ob"], w["out"]]
    full = lambda a: pl.BlockSpec(a.shape, lambda i: (0,) * a.ndim)
    out_shape = (jax.ShapeDtypeStruct((b, d), F32), jax.ShapeDtypeStruct((b, width), F32),
                 jax.ShapeDtypeStruct((b, width), F32))
    return pl.pallas_call(
        kern, grid=(1,), in_specs=[full(a) for a in args],
        out_specs=tuple(pl.BlockSpec(s.shape, lambda i: (0, 0)) for s in out_shape), out_shape=out_shape,
        compiler_params=_params(("arbitrary",)), name="sample_lru_merge",
    )(*args)


def _sample_score_kernel(pt_ref, qi_ref, wi_ref, kin_ref, cki_ref, o_ref, kbuf, sem,
                         *, layer_page0, n_pages, page, ch):
    b = pl.program_id(0)

    def page_copy(p):
        phys = pt_ref[b, p]
        return pltpu.make_async_copy(cki_ref.at[layer_page0 + phys], kbuf.at[pl.ds(p * page, page)], sem)

    def start(p, c):
        page_copy(p).start()
        return c

    def wait(p, c):
        page_copy(p).wait()
        return c

    lax.fori_loop(0, n_pages, start, 0)
    lax.fori_loop(0, n_pages, wait, 0)

    qi = qi_ref[0]
    w = wi_ref[0] * IDX_SCALE
    n_sub = ch // LANES

    def body(c, carry):
        start_row = pl.multiple_of(c * ch, ch)
        kc = kbuf[pl.ds(start_row, ch), :].astype(BF16)
        x = _dot_nt(qi, kc)
        s = jnp.sum(w * jnp.maximum(x, 0.0), axis=0, keepdims=True)
        for j in range(n_sub):
            o_ref[0, pl.ds(c * n_sub + j, 1), :] = s[:, j * LANES:(j + 1) * LANES]
        return carry

    lax.fori_loop(0, (n_pages * page) // ch, body, 0)

    kn = kin_ref[0].astype(BF16).astype(F32)
    xn = jnp.sum(qi.astype(F32) * kn, axis=-1, keepdims=True)
    sn = jnp.sum(w * jnp.maximum(xn, 0.0), axis=0, keepdims=True)
    n_rows = (n_pages * page) // LANES
    tail_rows = o_ref.shape[1] - n_rows
    r_i = lax.broadcasted_iota(I32, (tail_rows, LANES), 0)
    l_i = lax.broadcasted_iota(I32, (tail_rows, LANES), 1)
    o_ref[0, n_rows:n_rows + tail_rows, :] = jnp.where(jnp.logical_and(r_i == 0, l_i == 0), sn, -jnp.inf)


def _sample_scores(page_table, qi, wi, ki_new, cki, layer, n_pool, ch):
    b, n_pages = page_table.shape
    page = cki.shape[1]
    n_rows = pl.cdiv((n_pages * page) // LANES + 1, LANES) * LANES
    kern = functools.partial(_sample_score_kernel, layer_page0=layer * n_pool, n_pages=n_pages, page=page, ch=ch)
    grid_spec = pltpu.PrefetchScalarGridSpec(
        num_scalar_prefetch=1, grid=(b,),
        in_specs=[pl.BlockSpec((1, IDX_HEADS, IDX_DIM), lambda i, pt: (i, 0, 0)),
                  pl.BlockSpec((1, IDX_HEADS, 1), lambda i, pt: (i, 0, 0)),
                  pl.BlockSpec((1, 1, IDX_DIM), lambda i, pt: (i, 0, 0)),
                  pl.BlockSpec(memory_space=pl.ANY)],
        out_specs=pl.BlockSpec((1, n_rows, LANES), lambda i, pt: (i, 0, 0)),
        scratch_shapes=[pltpu.VMEM((n_pages * page, IDX_DIM), F32), pltpu.SemaphoreType.DMA(())],
    )
    return pl.pallas_call(
        kern, grid_spec=grid_spec, out_shape=jax.ShapeDtypeStruct((b, n_rows, LANES), F32),
        compiler_params=_params(("arbitrary",)), name="sample_scores",
    )(page_table, qi, wi, ki_new, cki)


def _sample_select_kernel(sc_ref, idx_ref, mask_sc, slot_sc, *, n_keys, k_top):
    b = pl.program_id(0)
    nb, n_rows, _ = sc_ref.shape

    @pl.when(b == 0)
    def _():
        x = sc_ref[...]
        kp = (lax.broadcasted_iota(I32, x.shape, 1) * LANES + lax.broadcasted_iota(I32, x.shape, 2))
        red = lambda a, op: op(op(a, axis=2, keepdims=True), axis=1, keepdims=True)
        amax = red(jnp.where(x == -jnp.inf, 0.0, jnp.abs(x)), jnp.max)
        count_ge = lambda t: red(jnp.where(x >= t, 1.0, 0.0), jnp.sum)
        min_ge = lambda t: red(jnp.where(x >= t, x, jnp.inf), jnp.min)
        count_gt_min_gt = lambda t: (red(jnp.where(x > t, 1.0, 0.0), jnp.sum),
                                     red(jnp.where(x > t, x, jnp.inf), jnp.min))
        count_tie_le = lambda t, jm: red(jnp.where(jnp.logical_and(x == t, kp <= jm), 1.0, 0.0), jnp.sum)
        need = jnp.full((nb, 1, 1), float(k_top), F32)
        t_val, j_val = _select_topk(count_ge, min_ge, count_gt_min_gt, count_tie_le, amax, need, n_keys)
        sel = jnp.logical_or(x > t_val, jnp.logical_and(x == t_val, kp <= j_val))
        mask_sc[...] = jnp.where(sel, 1.0, 0.0)

    mask = mask_sc[b]
    upper = (lax.broadcasted_iota(I32, (LANES, LANES), 0) <= lax.broadcasted_iota(I32, (LANES, LANES), 1))
    incl = _dot(mask.astype(BF16), jnp.where(upper, 1.0, 0.0).astype(BF16))
    row_tot = jnp.broadcast_to(incl[:, LANES - 1:LANES], (n_rows, LANES))
    strict = (lax.broadcasted_iota(I32, (n_rows, n_rows), 1) < lax.broadcasted_iota(I32, (n_rows, n_rows), 0))
    before = _dot(jnp.where(strict, 1.0, 0.0).astype(BF16), row_tot.astype(BF16))
    slot_sc[...] = jnp.where(mask > 0.0, incl + before - 1.0, -1.0)

    slot_iota = lax.broadcasted_iota(I32, (k_top, LANES), 0).astype(F32)
    lane = lax.broadcasted_iota(I32, (k_top, LANES), 1)

    def body(r, acc):
        srow = jnp.broadcast_to(slot_sc[pl.ds(r, 1), :], (k_top, LANES))
        return acc + jnp.where(srow == slot_iota, (r * LANES + lane).astype(F32), 0.0)

    acc = lax.fori_loop(0, pl.cdiv(n_keys, LANES), body, jnp.zeros((k_top, LANES), F32))
    idx_ref[0] = jnp.sum(acc, axis=-1, keepdims=True).astype(I32)


def _sample_select(scores, n_keys, k_top):
    b, n_rows, _ = scores.shape
    kern = functools.partial(_sample_select_kernel, n_keys=n_keys, k_top=k_top)
    return pl.pallas_call(
        kern, grid=(b,),
        in_specs=[pl.BlockSpec(scores.shape, lambda i: (0, 0, 0))],
        out_specs=pl.BlockSpec((1, k_top, 1), lambda i: (i, 0, 0)),
        out_shape=jax.ShapeDtypeStruct((b, k_top, 1), I32),
        scratch_shapes=[pltpu.VMEM(scores.shape, F32), pltpu.VMEM((n_rows, LANES), F32)],
        compiler_params=_params(("arbitrary",)), name="sample_select",
    )(scores)


def _sample_attn_kernel(idx_s, pt_s, idxv_ref, q_ref, kn_ref, vn_ref, ck_ref, cv_ref, o_ref,
                        kbuf, vbuf, sem, *, layer_row0, past, page, k_top):
    b = pl.program_id(0)

    def copies(j):
        pos = jnp.minimum(idx_s[b, j], past - 1)
        row = layer_row0 + pt_s[b, pos // page] * page + pos % page
        return (pltpu.make_async_copy(ck_ref.at[pl.ds(row, 1)], kbuf.at[pl.ds(j, 1)], sem.at[0]),
                pltpu.make_async_copy(cv_ref.at[pl.ds(row, 1)], vbuf.at[pl.ds(j, 1)], sem.at[1]))

    def start(j, c):
        ck, cv = copies(j)
        ck.start()
        cv.start()
        return c

    def wait(j, c):
        ck, cv = copies(j)
        ck.wait()
        cv.wait()
        return c

    lax.fori_loop(0, k_top, start, 0)
    lax.fori_loop(0, k_top, wait, 0)

    is_new = idxv_ref[0] >= past
    k_sel = jnp.where(is_new, kn_ref[0], kbuf[...]).astype(BF16)
    v_sel = jnp.where(is_new, vn_ref[0], vbuf[...]).astype(BF16)
    head_of_lane = lax.broadcasted_iota(I32, (N_HEADS, ATT_WIDTH), 1) // HEAD_DIM
    own = head_of_lane == lax.broadcasted_iota(I32, (N_HEADS, ATT_WIDTH), 0)
    q_rows = jnp.where(own, jnp.broadcast_to(q_ref[0].astype(F32), (N_HEADS, ATT_WIDTH)), 0.0).astype(BF16)
    s = _dot_nt(q_rows, k_sel)
    p = jnp.exp2(s - jnp.max(s, axis=-1, keepdims=True))
    l = jnp.sum(p, axis=-1, keepdims=True)
    o = _dot(p.astype(BF16), v_sel) / l
    o_ref[0] = jnp.sum(jnp.where(own, o, 0.0), axis=0, keepdims=True)


def _sample_attention(idx, page_table, q, k_new, v_new, ck, cv, layer, n_pool, page):
    b, k_top, _ = idx.shape
    past = page_table.shape[1] * page
    kern = functools.partial(_sample_attn_kernel, layer_row0=layer * n_pool * page, past=past, page=page,
                             k_top=k_top)
    per_b = lambda c: pl.BlockSpec((1, 1, c), lambda i, a, p: (i, 0, 0))
    grid_spec = pltpu.PrefetchScalarGridSpec(
        num_scalar_prefetch=2, grid=(b,),
        in_specs=[pl.BlockSpec((1, k_top, 1), lambda i, a, p: (i, 0, 0)),
                  per_b(ATT_WIDTH), per_b(ATT_WIDTH), per_b(ATT_WIDTH),
                  pl.BlockSpec(memory_space=pl.ANY), pl.BlockSpec(memory_space=pl.ANY)],
        out_specs=per_b(ATT_WIDTH),
        scratch_shapes=[pltpu.VMEM((k_top, ATT_WIDTH), F32), pltpu.VMEM((k_top, ATT_WIDTH), F32),
                        pltpu.SemaphoreType.DMA((2,))],
    )
    out = pl.pallas_call(
        kern, grid_spec=grid_spec, out_shape=jax.ShapeDtypeStruct((b, 1, ATT_WIDTH), F32),
        compiler_params=_params(("arbitrary",)), name="sample_attention",
    )(idx.reshape(b, k_top), page_table, idx, q.reshape(b, 1, ATT_WIDTH), k_new.reshape(b, 1, ATT_WIDTH),
      v_new.reshape(b, 1, ATT_WIDTH), ck, cv)
    return out.reshape(b, ATT_WIDTH)


def _norm_kernel(x_ref, g_ref, o_ref):
    o_ref[...] = _rmsnorm(x_ref[...], g_ref[...])


def _final_norm(x, g, tm):
    m, d = x.shape
    return pl.pallas_call(
        _norm_kernel, grid=(m // tm,),
        in_specs=[pl.BlockSpec((tm, d), lambda i: (i, 0)), pl.BlockSpec((1, d), lambda i: (0, 0))],
        out_specs=pl.BlockSpec((tm, d), lambda i: (i, 0)), out_shape=jax.ShapeDtypeStruct((m, d), F32),
        compiler_params=_params(("arbitrary",)), name="final_norm",
    )(x, g)


def _layer_weights(l, norm_g, w_in, conv_w, conv_b, w_rg, b_rg, w_ig, b_ig, lru_lam, w_oa, w_ob, w_out):
    d = w_in.shape[1]
    width = lru_lam.shape[1]
    wl = w_in[l]
    o_idx = 4 * ATT_WIDTH
    o_ki = o_idx + IDX_WIDTH
    o_wi = o_ki + IDX_DIM
    o_lru = o_wi + IDX_HEADS
    idx_cols = wl[:, o_idx:o_lru]
    idx_cols = jnp.pad(idx_cols, ((0, 0), (0, IDX_WIDTH + LANES - idx_cols.shape[1])))
    blockdiag = lambda w: jax.scipy.linalg.block_diag(*[w[n] for n in range(w.shape[0])])
    r2 = lambda v: v.reshape(1, -1)
    return {
        "g": r2(norm_g[l]),
        "qkvg": wl[:, 0:o_idx].astype(BF16),
        "kT": wl[:, ATT_WIDTH:2 * ATT_WIDTH].T.astype(BF16),
        "vT": wl[:, 2 * ATT_WIDTH:3 * ATT_WIDTH].T.astype(BF16),
        "idx": idx_cols.astype(BF16),
        "kiT": wl[:, o_ki:o_wi].T.astype(BF16),
        "lru": wl[:, o_lru:o_lru + 2 * width + 2 * d].astype(BF16),
        "conv_w": conv_w[l], "conv_b": r2(conv_b[l]),
        "rg": blockdiag(w_rg[l]).astype(BF16), "b_rg": r2(b_rg[l]),
        "ig": blockdiag(w_ig[l]).astype(BF16), "b_ig": r2(b_ig[l]),
        "lam": r2(lru_lam[l]),
        "oa": w_oa[l].astype(BF16), "ob": w_ob[l].astype(BF16), "out": w_out[l].astype(BF16),
    }


def _pick_tile(m, target):
    t = min(m, target)
    while m % t:
        t //= 2
    return t


def kernel(x_prompt, x_sample, cache_k, cache_v, cache_kidx, state_conv, state_h, page_table,
           norm_g, w_in, conv_w, conv_b, w_rg, b_rg, w_ig, b_ig, lru_lam, w_oa, w_ob, w_out, final_g):
    bp, seq, d = x_prompt.shape
    bs, dec_seq, _ = x_sample.shape
    depth, n_pool, page = cache_k.shape[0], cache_k.shape[1], cache_k.shape[2]
    assert bp == 1 and dec_seq == 1, "one prompt sequence and one new token per sampled sequence"
    n_pages = page_table.shape[1]
    past = n_pages * page
    k_top_s = min(TOP_K_MAX, (past + dec_seq) // 4)

    ck = cache_k.reshape(depth * n_pool * page, ATT_WIDTH)
    cv = cache_v.reshape(depth * n_pool * page, ATT_WIDTH)
    cki = cache_kidx.reshape(depth * n_pool, page, IDX_DIM)

    tm_proj = _pick_tile(seq, 256)
    tq = _pick_tile(seq, 128)
    ch = _pick_tile(seq, 512)
    ch_s = _pick_tile(past, 1024)

    xp = x_prompt.reshape(seq, d)
    xs = x_sample.reshape(bs, d)
    outs = {n: [] for n in ("pk", "pv", "pki", "pconv", "ph", "sk", "sv", "ski", "sconv", "sh")}
    for l in range(depth):
        w = _layer_weights(l, norm_g, w_in, conv_w, conv_b, w_rg, b_rg, w_ig, b_ig, lru_lam, w_oa, w_ob, w_out)
        qh, _, k, v, kT, vT, gatt, qi, ki, kiT, wi = _proj(xp, w["g"], w, tm_proj)
        ya = _prompt_attention(qh, qi, wi, kT, vT, kiT, tq, ch)
        xp, tail, hlast = _prompt_lru_merge(xp, ya, gatt, w, tm_proj)
        outs["pk"].append(k.reshape(1, seq, N_HEADS, HEAD_DIM))
        outs["pv"].append(v.reshape(1, seq, N_HEADS, HEAD_DIM))
        outs["pki"].append(ki.reshape(1, seq, IDX_DIM))
        outs["pconv"].append(tail[SUBLANES - (CONV_WIDTH - 1):].reshape(1, CONV_WIDTH - 1, -1))
        outs["ph"].append(hlast[0:1])
        _, qf, k, v, _, _, gatt, qi, ki, _, wi = _proj(xs, w["g"], w, bs)
        scores = _sample_scores(page_table, jnp.swapaxes(qi, 0, 1), wi.reshape(bs, IDX_HEADS, 1),
                                ki.reshape(bs, 1, IDX_DIM), cki, l, n_pool, ch_s)
        idx = _sample_select(scores, past + dec_seq, k_top_s)
        ya = _sample_attention(idx, page_table, qf, k, v, ck, cv, l, n_pool, page)
        xs, xl, h = _sample_lru_merge(xs, state_conv[l], state_h[l], ya, gatt, w)
        outs["sk"].append(k.reshape(bs, 1, N_HEADS, HEAD_DIM))
        outs["sv"].append(v.reshape(bs, 1, N_HEADS, HEAD_DIM))
        outs["ski"].append(ki.reshape(bs, 1, IDX_DIM))
        outs["sconv"].append(jnp.concatenate([state_conv[l][:, 1:], xl[:, None, :]], axis=1))
        outs["sh"].append(h)
    y_prompt = _final_norm(xp, final_g.reshape(1, d), tm_proj).reshape(1, seq, d)
    y_sample = _final_norm(xs, final_g.reshape(1, d), bs).reshape(bs, 1, d)
    st = lambda n: jnp.stack(outs[n])
    return (y_prompt, y_sample, st("pk"), st("pv"), st("pki"), st("pconv"), st("ph"),
            st("sk"), st("sv"), st("ski"), st("sconv"), st("sh"))
```

```python
import functools

import jax
import jax.numpy as jnp
from jax import lax
from jax.experimental import pallas as pl
from jax.experimental.pallas import tpu as pltpu

F32 = jnp.float32
BF16 = jnp.bfloat16
I32 = jnp.int32

N_HEADS = 8
HEAD_DIM = 64
ATT_WIDTH = N_HEADS * HEAD_DIM
IDX_HEADS = 8
IDX_DIM = 64
IDX_WIDTH = IDX_HEADS * IDX_DIM
TOP_K_MAX = 256
CONV_WIDTH = 4
LRU_C = 8.0
NORM_EPS = 1e-6
IDX_SCALE = (IDX_DIM ** -0.5) * (IDX_HEADS ** -0.5)
ATT_SCALE = HEAD_DIM ** -0.5
LOG2E = 1.4426950408889634
NEG_BIG = -1e30
LANES = 128
SUBLANES = 8
BF16_ROWS = 16
MXU_WIDTH = 256
VMEM_LIMIT_BYTES = 56 * 1024 * 1024

_NT = (((1,), (1,)), ((), ()))


def _dot(a, b):
    return jnp.dot(a, b, preferred_element_type=F32)


def _dot_nt(a, b):
    return lax.dot_general(a, b, _NT, preferred_element_type=F32)


def _rmsnorm(x, g):
    return (x * lax.rsqrt(jnp.mean(x * x, axis=-1, keepdims=True) + NORM_EPS)) * g


def _sigmoid(x):
    return 1.0 / (1.0 + jnp.exp(-x))


def _silu(x):
    return x * _sigmoid(x)


def _log1p(e):
    w = 1.0 + e
    return jnp.where(w == 1.0, e, jnp.log(w) * (e / (w - 1.0)))


def _expm1(x):
    u = jnp.exp(x)
    um1 = u - 1.0
    return jnp.where(um1 == 0.0, x, jnp.where(u == 0.0, -1.0, um1 * (x / jnp.log(u))))


def _const_spec(shape):
    nd = len(shape)
    return pl.BlockSpec(shape, lambda *_: (0,) * nd, pipeline_mode=pl.Buffered(1))


def _params(sem):
    return pltpu.CompilerParams(dimension_semantics=sem, vmem_limit_bytes=VMEM_LIMIT_BYTES)


def _proj_kernel(x_ref, g_ref, wT_ref, wk_ref, wg_ref, wki_ref,
                 qT_ref, kT_ref, vT_ref, qiT_ref, kiT_ref, wiT_ref, kTb_ref, vTb_ref, k_ref, ki_ref, gatt_ref):
    hb = _rmsnorm(x_ref[...], g_ref[...]).astype(BF16)
    zT = _dot_nt(wT_ref[...], hb)
    o_k, o_v, o_qi = ATT_WIDTH, 2 * ATT_WIDTH, 3 * ATT_WIDTH
    o_ki = o_qi + IDX_WIDTH
    o_wi = o_ki + IDX_DIM
    qT_ref[...] = (zT[0:o_k] * (ATT_SCALE * LOG2E)).astype(BF16)
    kT_ref[...] = zT[o_k:o_v]
    vT_ref[...] = zT[o_v:o_qi]
    kTb_ref[...] = zT[o_k:o_v].astype(BF16)
    vTb_ref[...] = zT[o_v:o_qi].astype(BF16)
    qiT_ref[...] = zT[o_qi:o_ki].astype(BF16)
    kiT_ref[...] = zT[o_ki:o_wi]
    wiT_ref[...] = zT[o_wi:o_wi + IDX_HEADS]
    k_ref[...] = _dot(hb, wk_ref[...]).astype(BF16)
    ki_ref[...] = _dot(hb, wki_ref[...]).astype(BF16)
    gatt_ref[...] = _dot(hb, wg_ref[...])


def _proj(x, g, w, tm):
    m, d = x.shape
    row = lambda c: pl.BlockSpec((tm, c), lambda i: (i, 0))
    col = lambda r: pl.BlockSpec((r, tm), lambda i: (0, i))
    outs = (
        ((ATT_WIDTH, m), BF16, col),
        ((ATT_WIDTH, m), F32, col),
        ((ATT_WIDTH, m), F32, col),
        ((IDX_WIDTH, m), BF16, col),
        ((IDX_DIM, m), F32, col),
        ((IDX_HEADS, m), F32, col),
        ((ATT_WIDTH, m), BF16, col),
        ((ATT_WIDTH, m), BF16, col),
        ((m, ATT_WIDTH), BF16, row),
        ((m, IDX_DIM), BF16, row),
        ((m, ATT_WIDTH), F32, row),
    )
    out_shape = tuple(jax.ShapeDtypeStruct(s, t) for s, t, _ in outs)
    out_specs = tuple(f(s[0]) if f is col else f(s[1]) for s, _, f in outs)
    consts = [g, w["attT"], w["k"], w["gatt"], w["ki"]]
    return pl.pallas_call(
        _proj_kernel, grid=(m // tm,), in_specs=[row(d)] + [_const_spec(c.shape) for c in consts],
        out_specs=out_specs, out_shape=out_shape,
        compiler_params=_params(("arbitrary",)), name="proj",
    )(x, *consts)


MAX_BISECT = 30


def _select_topk(count_ge, min_ge, count_gt_min_gt, count_tie_le, amax, need, n_keys):
    shape = need.shape
    big_j = jnp.full(shape, n_keys, I32)
    hi0 = 2.0 * amax + 1e-30
    lo0 = -hi0

    def n_open(done):
        return jnp.sum(1.0 - done)

    def bis_cond(st):
        it, _, _, _, _, nopen = st
        return jnp.logical_and(it < MAX_BISECT, nopen > 0.0)

    def bis_body(st):
        it, lo, hi, thr, done, _ = st
        mid = 0.5 * (lo + hi)
        c = count_ge(mid)
        hit = jnp.logical_and(c == need, done == 0.0)
        thr = jnp.where(hit, mid, thr)
        done = jnp.where(hit, 1.0, done)
        lo = jnp.where(c > need, mid, lo)
        hi = jnp.where(c < need, mid, hi)
        return it + 1, lo, hi, thr, done, n_open(done)

    zeros = jnp.zeros(shape, F32)
    st = (jnp.int32(0), lo0, hi0, zeros, zeros, jnp.float32(1.0))
    _, lo, _, thr, done, _ = lax.while_loop(bis_cond, bis_body, st)

    def walk_cond(st):
        return st[-1] > 0.0

    def walk_body(st):
        lo, t_val, j_val, rem, done, tie, _ = st
        t_cand = min_ge(lo)
        c_gt, next_val = count_gt_min_gt(t_cand)
        fin = jnp.logical_and(c_gt <= need, done == 0.0)
        t_val = jnp.where(fin, t_cand, t_val)
        has_tie = jnp.logical_and(fin, c_gt < need)
        j_val = jnp.where(jnp.logical_and(fin, c_gt == need), -1, j_val)
        rem = jnp.where(has_tie, need - c_gt, rem)
        tie = jnp.where(has_tie, 1.0, tie)
        done = jnp.where(fin, 1.0, done)
        lo = jnp.where(done == 0.0, next_val, lo)
        return lo, t_val, j_val, rem, done, tie, n_open(done)

    st = (lo, thr, big_j, zeros, done, zeros, n_open(done))
    _, t_val, j_val, rem, _, tie, _ = lax.while_loop(walk_cond, walk_body, st)

    def tie_cond(st):
        it, _, _, ntie = st
        return jnp.logical_and(it < 32, ntie > 0.0)

    def tie_body(st):
        it, jlo, jhi, ntie = st
        jm = lax.shift_right_arithmetic(jlo + jhi, 1)
        c = count_tie_le(t_val, jm)
        ge = c >= rem
        jhi = jnp.where(ge, jm, jhi)
        jlo = jnp.where(ge, jlo, jm)
        return it + 1, jlo, jhi, ntie

    st = (jnp.int32(0), jnp.full(shape, -1, I32), jnp.full(shape, n_keys - 1, I32), jnp.sum(tie))
    _, _, jhi, _ = lax.while_loop(tie_cond, tie_body, st)
    j_val = jnp.where(tie > 0.0, jhi, j_val)
    return t_val, j_val


def _fold_rows(v, op):
    rows = v.shape[0]
    while rows > SUBLANES:
        rows //= 2
        v = op(v[0:rows], v[rows:2 * rows])
    return v


def _prompt_attn_kernel(qT_ref, qiT_ref, wiT_ref, k_ref, vT_ref, ki_ref, o_ref,
                        sc_ref, m_ref, l_ref, acc_ref, s_ref, *, tq, ch, cha, cat, seq):
    i = pl.program_id(0)
    n_ch = ((i + 1) * tq + ch - 1) // ch
    q_pos = i * tq + lax.broadcasted_iota(I32, (1, tq), 1)
    row_iota = lax.broadcasted_iota(I32, (ch, tq), 0)
    pair = MXU_WIDTH // tq

    w_idx = wiT_ref[...] * IDX_SCALE
    qi_rhs = [jnp.concatenate([qiT_ref[(g * pair + p) * IDX_DIM:(g * pair + p + 1) * IDX_DIM, :]
                               for p in range(pair)], axis=1) for g in range(IDX_HEADS // pair)]

    def score_body(c, amax):
        for r in range(ch // cha):
            start = pl.multiple_of(c * ch + r * cha, cha)
            kic = ki_ref[pl.ds(start, cha), :]
            acc = jnp.zeros((cha, tq), F32)
            for g in range(IDX_HEADS // pair):
                x = _dot(kic, qi_rhs[g])
                for p in range(pair):
                    h = g * pair + p
                    acc = acc + w_idx[h:h + 1, :] * jnp.maximum(x[:, p * tq:(p + 1) * tq], 0.0)
            amax = jnp.maximum(amax, _fold_rows(jnp.abs(acc), jnp.maximum))
            sc_ref[pl.ds(start, cha), :] = jnp.where(start + row_iota[0:cha] <= q_pos, acc, -jnp.inf)
        return amax

    amax = lax.fori_loop(0, n_ch, score_body, jnp.zeros((SUBLANES, tq), F32))
    amax = jnp.max(amax, axis=0, keepdims=True)

    def key_reduce(fn, init, combine, final):
        def body(c, acc):
            start = pl.multiple_of(c * ch, ch)
            return combine(acc, _fold_rows(fn(sc_ref[pl.ds(start, ch), :], start + row_iota), combine))
        return final(lax.fori_loop(0, n_ch, body, jnp.full((SUBLANES, tq), init, F32)), axis=0, keepdims=True)

    count = lambda fn: key_reduce(fn, 0.0, jnp.add, jnp.sum)
    lowest = lambda fn: key_reduce(fn, jnp.inf, jnp.minimum, jnp.min)
    count_ge = lambda t: count(lambda x, _: jnp.where(x >= t, 1.0, 0.0))
    min_ge = lambda t: lowest(lambda x, _: jnp.where(x >= t, x, jnp.inf))
    count_gt_min_gt = lambda t: (count(lambda x, _: jnp.where(x > t, 1.0, 0.0)),
                                 lowest(lambda x, _: jnp.where(x > t, x, jnp.inf)))
    count_tie_le = lambda t, jm: count(
        lambda x, kp: jnp.where(jnp.logical_and(x == t, kp <= jm), 1.0, 0.0))

    need = jnp.minimum(q_pos + 1, min(TOP_K_MAX, seq // 4)).astype(F32)
    t_val, j_val = _select_topk(count_ge, min_ge, count_gt_min_gt, count_tie_le, amax, need, seq)

    m_ref[...] = jnp.full(m_ref.shape, NEG_BIG, F32)
    l_ref[...] = jnp.zeros(l_ref.shape, F32)
    acc_ref[...] = jnp.zeros(acc_ref.shape, F32)
    zero_q = jnp.zeros((HEAD_DIM, tq), BF16)
    q_rhs = []
    for g in range(N_HEADS // pair):
        blocks = []
        for p in range(pair):
            qh = qT_ref[(g * pair + p) * HEAD_DIM:(g * pair + p + 1) * HEAD_DIM, :]
            blocks.append(jnp.concatenate([qh if pp == p else zero_q for pp in range(pair)], axis=1))
        q_rhs.append(jnp.concatenate(blocks, axis=0))
    ones_rows = jnp.ones((BF16_ROWS, cat), BF16)

    def attn_body(c, carry):
        start = pl.multiple_of(c * cat, cat)
        x = sc_ref[pl.ds(start, cat), :]
        sel = jnp.logical_or(x > t_val, jnp.logical_and(x == t_val, start + row_iota[0:cat] <= j_val))
        bias = jnp.where(sel, 0.0, -jnp.inf)
        m_cur = []
        for g in range(N_HEADS // pair):
            kc = k_ref[pl.ds(start, cat), g * pair * HEAD_DIM:(g + 1) * pair * HEAD_DIM]
            s_all = _dot(kc, q_rhs[g])
            for p in range(pair):
                s = s_all[:, p * tq:(p + 1) * tq] + bias
                s_ref[g * pair + p] = s
                m_cur.append(jnp.max(_fold_rows(s, jnp.maximum), axis=0, keepdims=True))
        m_prev = m_ref[...]
        m_new = jnp.maximum(m_prev, jnp.concatenate(m_cur, axis=0))
        alpha = jnp.exp2(m_prev - m_new)
        m_ref[...] = m_new
        l_cur = []
        for h in range(N_HEADS):
            pr = jnp.exp2(s_ref[h] - m_new[h:h + 1, :]).astype(BF16)
            lhs = jnp.concatenate([vT_ref[h * HEAD_DIM:(h + 1) * HEAD_DIM, pl.ds(start, cat)], ones_rows], axis=0)
            pv = _dot(lhs, pr)
            rows = pl.ds(h * HEAD_DIM, HEAD_DIM)
            acc_ref[rows, :] = alpha[h:h + 1, :] * acc_ref[rows, :] + pv[0:HEAD_DIM]
            l_cur.append(pv[HEAD_DIM:HEAD_DIM + 1])
        l_ref[...] = alpha * l_ref[...] + jnp.concatenate(l_cur, axis=0)
        return carry

    lax.fori_loop(0, n_ch * (ch // cat), attn_body, 0)
    outs =[acc_ref[h * HEAD_DIM:(h + 1) * HEAD_DIM, :] / l_ref[h:h + 1, :] for h in range(N_HEADS)]
    o_ref[...] = jnp.concatenate(outs, axis=0).T


def _prompt_attention(qT, qiT, wiT, k, vT, ki, tq, ch):
    seq = qT.shape[1]
    cha = min(ch, MXU_WIDTH)
    cat = ch
    kern = functools.partial(_prompt_attn_kernel, tq=tq, ch=ch, cha=cha, cat=cat, seq=seq)
    col = lambda r: pl.BlockSpec((r, tq), lambda i: (0, i))
    return pl.pallas_call(
        kern, grid=(seq // tq,),
        in_specs=[col(ATT_WIDTH), col(IDX_WIDTH), col(IDX_HEADS),
                  _const_spec(k.shape), _const_spec(vT.shape), _const_spec(ki.shape)],
        out_specs=pl.BlockSpec((tq, ATT_WIDTH), lambda i: (i, 0)),
        out_shape=jax.ShapeDtypeStruct((seq, ATT_WIDTH), F32),
        scratch_shapes=[pltpu.VMEM((seq, tq), F32),
                        pltpu.VMEM((N_HEADS, tq), F32),
                        pltpu.VMEM((N_HEADS, tq), F32),
                        pltpu.VMEM((ATT_WIDTH, tq), F32),
                        pltpu.VMEM((N_HEADS, cat, tq), F32)],
        compiler_params=_params(("arbitrary",)), name="prompt_attention",
    )(qT, qiT, wiT, k, vT, ki)


def _lru_gates(xc, wrg_ref, brg_ref, wig_ref, big_ref, lam_ref):
    xcb = xc.astype(BF16)
    r = _sigmoid(_dot(xcb, wrg_ref[...]) + brg_ref[...])
    ig = _sigmoid(_dot(xcb, wig_ref[...]) + big_ref[...])
    nlam = -lam_ref[...]
    softplus = jnp.maximum(nlam, 0.0) + _log1p(jnp.exp(-jnp.abs(nlam)))
    log_a = (-LRU_C) * r * softplus
    a = jnp.exp(log_a)
    u = jnp.sqrt(-_expm1(2.0 * log_a)) * (ig * xc)
    return a, u


def _merge(x, ya, g_att, yl, g_lru, ga, gb, woa_ref, wob_ref, wout_ref):
    pa = _dot((ya * _silu(g_att)).astype(BF16), woa_ref[...])
    pb = _dot((yl * _silu(g_lru)).astype(BF16), wob_ref[...])
    m = _sigmoid(ga) * pa + _sigmoid(gb) * pb
    return x + _dot(m.astype(BF16), wout_ref[...])


def _prompt_lru_kernel(x_ref, g_ref, wlru_ref, cw_ref, cb_ref, wrg_ref, brg_ref, wig_ref, big_ref, lam_ref,
                       ya_ref, gatt_ref, woa_ref, wob_ref, wout_ref,
                       xo_ref, tail_ref, hlast_ref, xl_sc, h_sc, *, tm, width):
    i = pl.program_id(0)

    @pl.when(i == 0)
    def _():
        xl_sc[0:SUBLANES, :] = jnp.zeros((SUBLANES, width), F32)
        h_sc[...] = jnp.zeros(h_sc.shape, F32)

    x = x_ref[...]
    hb = _rmsnorm(x, g_ref[...]).astype(BF16)
    z = _dot(hb, wlru_ref[...])
    xl_sc[SUBLANES:SUBLANES + tm, :] = z[:, 0:width]
    xc = cb_ref[...]
    for j in range(CONV_WIDTH):
        xc = xc + cw_ref[j:j + 1, :] * xl_sc[pl.ds(SUBLANES - (CONV_WIDTH - 1) + j, tm), :]
    a, b = _lru_gates(xc, wrg_ref, brg_ref, wig_ref, big_ref, lam_ref)

    t_idx = lax.broadcasted_iota(I32, (tm, width), 0)
    d = 1
    while d < tm:
        a_s = pltpu.roll(a, d, 0)
        b_s = pltpu.roll(b, d, 0)
        live = t_idx >= d
        b = jnp.where(live, a * b_s + b, b)
        a = jnp.where(live, a * a_s, a)
        d *= 2
    hs = a * h_sc[0:1, :] + b
    h_sc[...] = jnp.broadcast_to(hs[tm - 1:tm, :], h_sc.shape)
    hlast_ref[...] = h_sc[...]
    tail_ref[...] = xl_sc[tm:tm + SUBLANES, :]
    xl_sc[0:SUBLANES, :] = xl_sc[tm:tm + SUBLANES, :]

    xo_ref[...] = _merge(x, ya_ref[...], gatt_ref[...], hs, z[:, width:2 * width],
                         z[:, 2 * width:3 * width], z[:, 3 * width:4 * width], woa_ref, wob_ref, wout_ref)


def _prompt_lru_merge(x, ya, gatt, w, tm):
    m, d = x.shape
    width = w["rg"].shape[0]
    kern = functools.partial(_prompt_lru_kernel, tm=tm, width=width)
    row = lambda c: pl.BlockSpec((tm, c), lambda i: (i, 0))
    consts = [w["g"], w["lru"], w["conv_w"], w["conv_b"], w["rg"], w["b_rg"], w["ig"], w["b_ig"], w["lam"]]
    consts2 = [w["oa"], w["ob"], w["out"]]
    state = pl.BlockSpec((SUBLANES, width), lambda i: (0, 0))
    return pl.pallas_call(
        kern, grid=(m // tm,),
        in_specs=[row(d)] + [_const_spec(c.shape) for c in consts] + [row(ATT_WIDTH), row(ATT_WIDTH)]
                 + [_const_spec(c.shape) for c in consts2],
        out_specs=(row(d), state, state),
        out_shape=(jax.ShapeDtypeStruct((m, d), F32), jax.ShapeDtypeStruct((SUBLANES, width), F32),
                   jax.ShapeDtypeStruct((SUBLANES, width), F32)),
        scratch_shapes=[pltpu.VMEM((tm + SUBLANES, width), F32), pltpu.VMEM((SUBLANES, width), F32)],
        compiler_params=_params(("arbitrary",)), name="prompt_lru_merge",
    )(x, *consts, ya, gatt, *consts2)


def _sample_lru_kernel(x_ref, g_ref, wlru_ref, cw_ref, cb_ref, wrg_ref, brg_ref, wig_ref, big_ref, lam_ref,
                       c0_ref, c1_ref, c2_ref, h0_ref, ya_ref, gatt_ref, woa_ref, wob_ref, wout_ref,
                       xo_ref, xl_ref, h_ref, *, width):
    x = x_ref[...]
    hb = _rmsnorm(x, g_ref[...]).astype(BF16)
    z = _dot(hb, wlru_ref[...])
    xl = z[:, 0:width]
    xc = cb_ref[...]
    for j, past in enumerate((c0_ref, c1_ref, c2_ref)):
        xc = xc + cw_ref[j:j + 1, :] * past[...]
    xc = xc + cw_ref[CONV_WIDTH - 1:CONV_WIDTH, :] * xl
    a, u = _lru_gates(xc, wrg_ref, brg_ref, wig_ref, big_ref, lam_ref)
    h = u + a * h0_ref[...]
    xl_ref[...] = xl
    h_ref[...] = h
    xo_ref[...] = _merge(x, ya_ref[...], gatt_ref[...], h, z[:, width:2 * width],
                         z[:, 2 * width:3 * width], z[:, 3 * width:4 * width], woa_ref, wob_ref, wout_ref)


def _sample_lru_merge(x, conv_state, h0, ya, gatt, w):
    b, d = x.shape
    width = w["rg"].shape[0]
    kern = functools.partial(_sample_lru_kernel, width=width)
    args = [x, w["g"], w["lru"], w["conv_w"], w["conv_b"], w["rg"], w["b_rg"], w["ig"], w["b_ig"], w["lam"],
            conv_state[:, 0], conv_state[:, 1], conv_state[:, 2], h0, ya, gatt, w["oa"], w["ob"], w["out"]]
    full = lambda a: pl.BlockSpec(a.shape, lambda i: (0,) * a.ndim)
    out_shape = (jax.ShapeDtypeStruct((b, d), F32), jax.ShapeDtypeStruct((b, width), F32),
                 jax.ShapeDtypeStruct((b, width), F32))
    return pl.pallas_call(
        kern, grid=(1,), in_specs=[full(a) for a in args],
        out_specs=tuple(pl.BlockSpec(s.shape, lambda i: (0, 0)) for s in out_shape), out_shape=out_shape,
        compiler_params=_params(("arbitrary",)), name="sample_lru_merge",
    )(*args)


def _sample_score_kernel(pt_ref, qi_ref, wi_ref, kin_ref, cki_ref, o_ref, kbuf, sem,
                         *, layer_page0, n_pages, page, ch):
    b = pl.program_id(0)

    def page_copy(p):
        phys = pt_ref[b, p]
        return pltpu.make_async_copy(cki_ref.at[layer_page0 + phys],
                                     kbuf.at[:, pl.ds(pl.multiple_of(p * page, page), page)], sem)

    def start(p, c):
        page_copy(p).start()
        return c

    def wait(p, c):
        page_copy(p).wait()
        return c

    lax.fori_loop(0, n_pages, start, 0)
    lax.fori_loop(0, n_pages, wait, 0)

    qi = qi_ref[0]
    w = wi_ref[0] * IDX_SCALE
    n_sub = ch // LANES

    def body(c, carry):
        start_col = pl.multiple_of(c * ch, ch)
        x = _dot(qi, kbuf[:, pl.ds(start_col, ch)].astype(BF16))
        s = jnp.sum(w * jnp.maximum(x, 0.0), axis=0, keepdims=True)
        for j in range(n_sub):
            o_ref[0, pl.ds(c * n_sub + j, 1), :] = s[:, j * LANES:(j + 1) * LANES]
        return carry

    lax.fori_loop(0, (n_pages * page) // ch, body, 0)

    kn = kin_ref[0].astype(BF16).astype(F32)
    xn = jnp.sum(qi.astype(F32) * kn, axis=-1, keepdims=True)
    sn = jnp.sum(w * jnp.maximum(xn, 0.0), axis=0, keepdims=True)
    n_rows = (n_pages * page) // LANES
    tail_rows = o_ref.shape[1] - n_rows
    r_i = lax.broadcasted_iota(I32, (tail_rows, LANES), 0)
    l_i = lax.broadcasted_iota(I32, (tail_rows, LANES), 1)
    o_ref[0, n_rows:n_rows + tail_rows, :] = jnp.where(jnp.logical_and(r_i == 0, l_i == 0), sn, -jnp.inf)


def _sample_scores(page_table, qi, wi, ki_new, ckiT, layer, n_pool, ch):
    b, n_pages = page_table.shape
    page = ckiT.shape[2]
    n_rows = pl.cdiv((n_pages * page) // LANES + 1, SUBLANES) * SUBLANES
    kern = functools.partial(_sample_score_kernel, layer_page0=layer * n_pool, n_pages=n_pages, page=page, ch=ch)
    grid_spec = pltpu.PrefetchScalarGridSpec(
        num_scalar_prefetch=1, grid=(b,),
        in_specs=[pl.BlockSpec((1, IDX_HEADS, IDX_DIM), lambda i, pt: (i, 0, 0)),
                  pl.BlockSpec((1, IDX_HEADS, 1), lambda i, pt: (i, 0, 0)),
                  pl.BlockSpec((1, 1, IDX_DIM), lambda i, pt: (i, 0, 0)),
                  pl.BlockSpec(memory_space=pl.ANY)],
        out_specs=pl.BlockSpec((1, n_rows, LANES), lambda i, pt: (i, 0, 0)),
        scratch_shapes=[pltpu.VMEM((IDX_DIM, n_pages * page), F32), pltpu.SemaphoreType.DMA(())],
    )
    return pl.pallas_call(
        kern, grid_spec=grid_spec, out_shape=jax.ShapeDtypeStruct((b, n_rows, LANES), F32),
        compiler_params=_params(("arbitrary",)), name="sample_scores",
    )(page_table, qi, wi, ki_new, ckiT)


def _sample_select_kernel(sc_ref, bias_ref, *, n_keys, k_top):
    x = sc_ref[...]
    nb = x.shape[0]
    kp = (lax.broadcasted_iota(I32, x.shape, 1) * LANES + lax.broadcasted_iota(I32, x.shape, 2))
    red = lambda a, op: op(op(a, axis=2, keepdims=True), axis=1, keepdims=True)
    amax = red(jnp.where(x == -jnp.inf, 0.0, jnp.abs(x)), jnp.max)
    count_ge = lambda t: red(jnp.where(x >= t, 1.0, 0.0), jnp.sum)
    min_ge = lambda t: red(jnp.where(x >= t, x, jnp.inf), jnp.min)
    count_gt_min_gt = lambda t: (red(jnp.where(x > t, 1.0, 0.0), jnp.sum),
                                 red(jnp.where(x > t, x, jnp.inf), jnp.min))
    count_tie_le = lambda t, jm: red(jnp.where(jnp.logical_and(x == t, kp <= jm), 1.0, 0.0), jnp.sum)
    need = jnp.full((nb, 1, 1), float(k_top), F32)
    t_val, j_val = _select_topk(count_ge, min_ge, count_gt_min_gt, count_tie_le, amax, need, n_keys)
    sel = jnp.logical_or(x > t_val, jnp.logical_and(x == t_val, kp <= j_val))
    bias_ref[...] = jnp.where(sel, 0.0, -jnp.inf)


def _sample_select(scores, n_keys, k_top):
    kern = functools.partial(_sample_select_kernel, n_keys=n_keys, k_top=k_top)
    full = pl.BlockSpec(scores.shape, lambda i: (0, 0, 0))
    return pl.pallas_call(
        kern, grid=(1,), in_specs=[full], out_specs=full,
        out_shape=jax.ShapeDtypeStruct(scores.shape, F32),
        compiler_params=_params(("arbitrary",)), name="sample_select",
    )(scores)


def _sample_attn_kernel(pt_ref, bias_ref, q_ref, kn_ref, vn_ref, ck_ref, cv_ref, o_ref,
                        kbuf, vbuf, acc_sc, sem, *, layer_page0, n_pages, page, group):
    b = pl.program_id(0)
    n_groups = n_pages // group

    def copies(g, slot):
        out = []
        for j in range(group):
            phys = pt_ref[b, g * group + j]
            out.append(pltpu.make_async_copy(ck_ref.at[layer_page0 + phys], kbuf.at[slot, j], sem.at[0, slot]))
            out.append(pltpu.make_async_copy(cv_ref.at[layer_page0 + phys], vbuf.at[slot, j], sem.at[1, slot]))
        return out

    def start(g, slot):
        for cp in copies(g, slot):
            cp.start()

    def wait(g, slot):
        for cp in copies(g, slot):
            cp.wait()

    q = q_ref[0]
    qb = jnp.broadcast_to(q, (N_HEADS, HEAD_DIM, page))
    acc_sc[...] = jnp.zeros(acc_sc.shape, F32)
    start(0, 0)

    def body(g, carry):
        m_prev, l_prev = carry
        slot = g % 2
        wait(g, slot)

        @pl.when(g + 1 < n_groups)
        def _():
            start(g + 1, 1 - slot)

        s = []
        for j in range(group):
            kt = kbuf[slot, j]
            sj = jnp.sum(kt * qb, axis=1, keepdims=True)
            s.append(sj + bias_ref[0, pl.ds(g * group + j, 1), :][None])
        m_cur = s[0]
        for sj in s[1:]:
            m_cur = jnp.maximum(m_cur, sj)
        m_new = jnp.maximum(m_prev, jnp.max(m_cur, axis=2, keepdims=True))
        alpha = jnp.exp2(m_prev - m_new)
        pv = jnp.zeros((N_HEADS, HEAD_DIM, page), F32)
        l_cur = jnp.zeros((N_HEADS, 1, page), F32)
        for j in range(group):
            pj = jnp.exp2(s[j] - m_new)
            l_cur = l_cur + pj
            pv = pv + vbuf[slot, j] * pj
        acc_sc[...] = alpha * acc_sc[...] + pv
        return m_new, alpha * l_prev + jnp.sum(l_cur, axis=2, keepdims=True)

    init = (jnp.full((N_HEADS, 1, 1), NEG_BIG, F32), jnp.zeros((N_HEADS, 1, 1), F32))
    m_run, l_run = lax.fori_loop(0, n_groups, body, init)

    s_new = (jnp.sum(kn_ref[0] * q, axis=1, keepdims=True)
             + bias_ref[0, n_pages:n_pages + 1, 0:1][None])
    m_fin = jnp.maximum(m_run, s_new)
    alpha = jnp.exp2(m_run - m_fin)
    p_new = jnp.exp2(s_new - m_fin)
    l_fin = alpha * l_run + p_new
    out = (alpha * jnp.sum(acc_sc[...], axis=2, keepdims=True) + p_new * vn_ref[0]) / l_fin
    o_ref[0] = out


def _sample_attention(bias, page_table, q, k_new, v_new, ckT, cvT, layer, n_pool, group):
    b, n_pages = page_table.shape
    page = ckT.shape[-1]
    kern = functools.partial(_sample_attn_kernel, layer_page0=layer * n_pool, n_pages=n_pages, page=page,
                             group=group)
    per_b = pl.BlockSpec((1, N_HEADS, HEAD_DIM, 1), lambda i, p: (i, 0, 0, 0))
    grid_spec = pltpu.PrefetchScalarGridSpec(
        num_scalar_prefetch=1, grid=(b,),
        in_specs=[pl.BlockSpec((1,) + bias.shape[1:], lambda i, p: (i, 0, 0)),
                  per_b, per_b, per_b,
                  pl.BlockSpec(memory_space=pl.ANY), pl.BlockSpec(memory_space=pl.ANY)],
        out_specs=per_b,
        scratch_shapes=[pltpu.VMEM((2, group, N_HEADS, HEAD_DIM, page), F32),
                        pltpu.VMEM((2, group, N_HEADS, HEAD_DIM, page), F32),
                        pltpu.VMEM((N_HEADS, HEAD_DIM, page), F32),
                        pltpu.SemaphoreType.DMA((2, 2))],
    )
    col = lambda a: a.reshape(b, N_HEADS, HEAD_DIM, 1)
    out = pl.pallas_call(
        kern, grid_spec=grid_spec, out_shape=jax.ShapeDtypeStruct((b, N_HEADS, HEAD_DIM, 1), F32),
        compiler_params=_params(("arbitrary",)), name="sample_attention",
    )(page_table, bias, col(q), col(k_new), col(v_new), ckT, cvT)
    return out.reshape(b, ATT_WIDTH)


def _norm_kernel(x_ref, g_ref, o_ref):
    o_ref[...] = _rmsnorm(x_ref[...], g_ref[...])


def _final_norm(x, g, tm):
    m, d = x.shape
    return pl.pallas_call(
        _norm_kernel, grid=(m // tm,),
        in_specs=[pl.BlockSpec((tm, d), lambda i: (i, 0)), pl.BlockSpec((1, d), lambda i: (0, 0))],
        out_specs=pl.BlockSpec((tm, d), lambda i: (i, 0)), out_shape=jax.ShapeDtypeStruct((m, d), F32),
        compiler_params=_params(("arbitrary",)), name="final_norm",
    )(x, g)


def _layer_weights(l, norm_g, w_in, conv_w, conv_b, w_rg, b_rg, w_ig, b_ig, lru_lam, w_oa, w_ob, w_out):
    d = w_in.shape[1]
    width = lru_lam.shape[1]
    wl = w_in[l]
    o_g = 3 * ATT_WIDTH
    o_qi = 4 * ATT_WIDTH
    o_ki = o_qi + IDX_WIDTH
    o_wi = o_ki + IDX_DIM
    o_lru = o_wi + IDX_HEADS
    att_t = jnp.concatenate([wl[:, 0:o_g], wl[:, o_qi:o_lru]], axis=1).T
    att_t = jnp.pad(att_t, ((0, (-att_t.shape[0]) % BF16_ROWS), (0, 0)))
    blockdiag = lambda w: jax.scipy.linalg.block_diag(*[w[n] for n in range(w.shape[0])])
    r2 = lambda v: v.reshape(1, -1)
    return {
        "g": r2(norm_g[l]),
        "attT": att_t.astype(BF16),
        "k": wl[:, ATT_WIDTH:2 * ATT_WIDTH].astype(BF16),
        "gatt": wl[:, o_g:o_qi].astype(BF16),
        "ki": wl[:, o_ki:o_wi].astype(BF16),
        "lru": wl[:, o_lru:o_lru + 2 * width + 2 * d].astype(BF16),
        "conv_w": conv_w[l], "conv_b": r2(conv_b[l]),
        "rg": blockdiag(w_rg[l]).astype(BF16), "b_rg": r2(b_rg[l]),
        "ig": blockdiag(w_ig[l]).astype(BF16), "b_ig": r2(b_ig[l]),
        "lam": r2(lru_lam[l]),
        "oa": w_oa[l].astype(BF16), "ob": w_ob[l].astype(BF16), "out": w_out[l].astype(BF16),
    }


def _pick_tile(m, target):
    t = min(m, target)
    while m % t:
        t //= 2
    return t


def kernel(x_prompt, x_sample, cache_k, cache_v, cache_kidx, state_conv, state_h, page_table,
           norm_g, w_in, conv_w, conv_b, w_rg, b_rg, w_ig, b_ig, lru_lam, w_oa, w_ob, w_out, final_g):
    bp, seq, d = x_prompt.shape
    bs, dec_seq, _ = x_sample.shape
    depth, n_pool, page = cache_k.shape[0], cache_k.shape[1], cache_k.shape[2]
    assert bp == 1 and dec_seq == 1, "one prompt sequence and one new token per sampled sequence"
    assert page == LANES, "a cache page fills the lane axis"
    n_pages = page_table.shape[1]
    past = n_pages * page
    k_top_s = min(TOP_K_MAX, (past + dec_seq) // 4)

    ckT = jnp.transpose(cache_k, (0, 1, 3, 4, 2)).reshape(depth * n_pool, N_HEADS, HEAD_DIM, page)
    cvT = jnp.transpose(cache_v, (0, 1, 3, 4, 2)).reshape(depth * n_pool, N_HEADS, HEAD_DIM, page)
    ckiT = jnp.transpose(cache_kidx, (0, 1, 3, 2)).reshape(depth * n_pool, IDX_DIM, page)

    tm_proj = _pick_tile(seq, 256)
    tq = _pick_tile(seq, 128)
    ch = _pick_tile(seq, 512)
    ch_s = _pick_tile(past, 1024)
    group = _pick_tile(n_pages, 8)

    xp = x_prompt.reshape(seq, d)
    xs = x_sample.reshape(bs, d)
    outs = {n: [] for n in ("pk", "pv", "pki", "pconv", "ph", "sk", "sv", "ski", "sconv", "sh")}
    heads_last = lambda t, n: jnp.transpose(t.reshape(N_HEADS, HEAD_DIM, n), (2, 0, 1))
    for l in range(depth):
        w = _layer_weights(l, norm_g, w_in, conv_w, conv_b, w_rg, b_rg, w_ig, b_ig, lru_lam, w_oa, w_ob, w_out)
        qT, kT, vT, qiT, kiT, wiT, _, vTb, kb, kib, gatt = _proj(xp, w["g"], w, tm_proj)
        ya = _prompt_attention(qT, qiT, wiT, kb, vTb, kib, tq, ch)
        xp, tail, hlast = _prompt_lru_merge(xp, ya, gatt, w, tm_proj)
        outs["pk"].append(heads_last(kT, seq)[None])
        outs["pv"].append(heads_last(vT, seq)[None])
        outs["pki"].append(kiT.T[None])
        outs["pconv"].append(tail[SUBLANES - (CONV_WIDTH - 1):].reshape(1, CONV_WIDTH - 1, -1))
        outs["ph"].append(hlast[0:1])
        qT, kT, vT, qiT, kiT, wiT, _, _, _, _, gatt = _proj(xs, w["g"], w, bs)
        k_new, v_new, ki_new = kT.T, vT.T, kiT.T
        scores = _sample_scores(page_table, qiT.T.reshape(bs, IDX_HEADS, IDX_DIM), wiT.T.reshape(bs, IDX_HEADS, 1),
                                ki_new.reshape(bs, 1, IDX_DIM), ckiT, l, n_pool, ch_s)
        bias = _sample_select(scores, past + dec_seq, k_top_s)
        ya = _sample_attention(bias, page_table, qT.T.astype(F32), k_new, v_new, ckT, cvT, l, n_pool, group)
        xs, xl, h = _sample_lru_merge(xs, state_conv[l], state_h[l], ya, gatt, w)
        outs["sk"].append(k_new.reshape(bs, 1, N_HEADS, HEAD_DIM))
        outs["sv"].append(v_new.reshape(bs, 1, N_HEADS, HEAD_DIM))
        outs["ski"].append(ki_new.reshape(bs, 1, IDX_DIM))
        outs["sconv"].append(jnp.concatenate([state_conv[l][:, 1:], xl[:, None, :]], axis=1))
        outs["sh"].append(h)
    y_prompt = _final_norm(xp, final_g.reshape(1, d), tm_proj).reshape(1, seq, d)
    y_sample = _final_norm(xs, final_g.reshape(1, d), bs).reshape(bs, 1, d)
    st = lambda n: jnp.stack(outs[n])
    return (y_prompt, y_sample, st("pk"), st("pv"), st("pki"), st("pconv"), st("ph"),
            st("sk"), st("sv"), st("ski"), st("sconv"), st("sh"))
```

```python
import functools

import jax
import jax.numpy as jnp
from jax import lax
from jax.experimental import pallas as pl
from jax.experimental.pallas import tpu as pltpu

F32 = jnp.float32
BF16 = jnp.bfloat16
I32 = jnp.int32

N_HEADS = 8
HEAD_DIM = 64
ATT_WIDTH = N_HEADS * HEAD_DIM
IDX_HEADS = 8
IDX_DIM = 64
IDX_WIDTH = IDX_HEADS * IDX_DIM
TOP_K_MAX = 256
CONV_WIDTH = 4
LRU_C = 8.0
NORM_EPS = 1e-6
IDX_SCALE = (IDX_DIM ** -0.5) * (IDX_HEADS ** -0.5)
ATT_SCALE = HEAD_DIM ** -0.5
LOG2E = 1.4426950408889634
NEG_BIG = -1e30
LANES = 128
SUBLANES = 8
BF16_ROWS = 16
MXU_WIDTH = 256
VMEM_LIMIT_BYTES = 56 * 1024 * 1024

_NT = (((1,), (1,)), ((), ()))


def _dot(a, b):
    return jnp.dot(a, b, preferred_element_type=F32)


def _dot_nt(a, b):
    return lax.dot_general(a, b, _NT, preferred_element_type=F32)


def _rmsnorm(x, g):
    return (x * lax.rsqrt(jnp.mean(x * x, axis=-1, keepdims=True) + NORM_EPS)) * g


def _sigmoid(x):
    return 1.0 / (1.0 + jnp.exp(-x))


def _silu(x):
    return x * _sigmoid(x)


def _log1p(e):
    w = 1.0 + e
    return jnp.where(w == 1.0, e, jnp.log(w) * (e / (w - 1.0)))


def _expm1(x):
    u = jnp.exp(x)
    um1 = u - 1.0
    return jnp.where(um1 == 0.0, x, jnp.where(u == 0.0, -1.0, um1 * (x / jnp.log(u))))


def _const_spec(shape):
    nd = len(shape)
    return pl.BlockSpec(shape, lambda *_: (0,) * nd, pipeline_mode=pl.Buffered(1))


def _params(sem):
    return pltpu.CompilerParams(dimension_semantics=sem, vmem_limit_bytes=VMEM_LIMIT_BYTES)


def _proj_kernel(x_ref, g_ref, wT_ref, wk_ref, wg_ref, wki_ref,
                 qT_ref, kT_ref, vT_ref, qiT_ref, kiT_ref, wiT_ref, kTb_ref, vTb_ref, k_ref, ki_ref, gatt_ref):
    hb = _rmsnorm(x_ref[...], g_ref[...]).astype(BF16)
    zT = _dot_nt(wT_ref[...], hb)
    o_k, o_v, o_qi = ATT_WIDTH, 2 * ATT_WIDTH, 3 * ATT_WIDTH
    o_ki = o_qi + IDX_WIDTH
    o_wi = o_ki + IDX_DIM
    qT_ref[...] = (zT[0:o_k] * (ATT_SCALE * LOG2E)).astype(BF16)
    kT_ref[...] = zT[o_k:o_v]
    vT_ref[...] = zT[o_v:o_qi]
    kTb_ref[...] = zT[o_k:o_v].astype(BF16)
    vTb_ref[...] = zT[o_v:o_qi].astype(BF16)
    qiT_ref[...] = zT[o_qi:o_ki].astype(BF16)
    kiT_ref[...] = zT[o_ki:o_wi]
    wiT_ref[...] = zT[o_wi:o_wi + IDX_HEADS]
    k_ref[...] = _dot(hb, wk_ref[...]).astype(BF16)
    ki_ref[...] = _dot(hb, wki_ref[...]).astype(BF16)
    gatt_ref[...] = _dot(hb, wg_ref[...])


def _proj(x, g, w, tm):
    m, d = x.shape
    row = lambda c: pl.BlockSpec((tm, c), lambda i: (i, 0))
    col = lambda r: pl.BlockSpec((r, tm), lambda i: (0, i))
    outs = (
        ((ATT_WIDTH, m), BF16, col),
        ((ATT_WIDTH, m), F32, col),
        ((ATT_WIDTH, m), F32, col),
        ((IDX_WIDTH, m), BF16, col),
        ((IDX_DIM, m), F32, col),
        ((IDX_HEADS, m), F32, col),
        ((ATT_WIDTH, m), BF16, col),
        ((ATT_WIDTH, m), BF16, col),
        ((m, ATT_WIDTH), BF16, row),
        ((m, IDX_DIM), BF16, row),
        ((m, ATT_WIDTH), F32, row),
    )
    out_shape = tuple(jax.ShapeDtypeStruct(s, t) for s, t, _ in outs)
    out_specs = tuple(f(s[0]) if f is col else f(s[1]) for s, _, f in outs)
    consts = [g, w["attT"], w["k"], w["gatt"], w["ki"]]
    return pl.pallas_call(
        _proj_kernel, grid=(m // tm,), in_specs=[row(d)] + [_const_spec(c.shape) for c in consts],
        out_specs=out_specs, out_shape=out_shape,
        compiler_params=_params(("arbitrary",)), name="proj",
    )(x, *consts)


MAX_BISECT = 24
PARK_SPAN = 8.0


def _select_topk(count_ge, min_ge, count_gt_min_gt, count_tie_le, amax, need, n_valid, n_keys):
    shape = need.shape
    big_j = jnp.full(shape, n_keys, I32)
    hi0 = 2.0 * amax + 1e-30
    lo0 = -hi0

    def n_open(done):
        return jnp.sum(1.0 - done)

    def bis_cond(st):
        return jnp.logical_and(st[0] < MAX_BISECT, st[-1] > 0.0)

    def bis_body(st):
        it, lo, hi, c_lo, c_hi, stale, thr, done, parked, _ = st
        probe = lo + 0.5 * (hi - lo)
        c = count_ge(probe)
        active = jnp.logical_and(done == 0.0, parked == 0.0)
        hit = jnp.logical_and(c == need, active)
        thr = jnp.where(hit, probe, thr)
        done = jnp.where(hit, 1.0, done)
        stale = jnp.where(jnp.logical_or(c == c_lo, c == c_hi), stale + 1.0, 0.0)
        up = jnp.logical_and(c > need, active)
        down = jnp.logical_and(c < need, active)
        lo, c_lo = jnp.where(up, probe, lo), jnp.where(up, c, c_lo)
        hi, c_hi = jnp.where(down, probe, hi), jnp.where(down, c, c_hi)
        give_up = jnp.logical_and(stale >= 2.0, c_lo - c_hi <= PARK_SPAN)
        parked = jnp.where(jnp.logical_and(give_up, done == 0.0), 1.0, parked)
        return it + 1, lo, hi, c_lo, c_hi, stale, thr, done, parked, jnp.sum((1.0 - done) * (1.0 - parked))

    zeros = jnp.zeros(shape, F32)
    st = (jnp.int32(0), lo0, hi0, n_valid, zeros, zeros, zeros, zeros, zeros, jnp.float32(1.0))
    _, lo, _, _, _, _, thr, done, _, _ = lax.while_loop(bis_cond, lambda s: bis_body(bis_body(s)), st)

    def walk_cond(st):
        return st[-1] > 0.0

    def walk_body(st):
        lo, t_val, j_val, rem, done, tie, _ = st
        t_cand = min_ge(lo)
        c_gt, next_val = count_gt_min_gt(t_cand)
        fin = jnp.logical_and(c_gt <= need, done == 0.0)
        t_val = jnp.where(fin, t_cand, t_val)
        has_tie = jnp.logical_and(fin, c_gt < need)
        j_val = jnp.where(jnp.logical_and(fin, c_gt == need), -1, j_val)
        rem = jnp.where(has_tie, need - c_gt, rem)
        tie = jnp.where(has_tie, 1.0, tie)
        done = jnp.where(fin, 1.0, done)
        lo = jnp.where(done == 0.0, next_val, lo)
        return lo, t_val, j_val, rem, done, tie, n_open(done)

    st = (lo, thr, big_j, zeros, done, zeros, n_open(done))
    _, t_val, j_val, rem, _, tie, _ = lax.while_loop(walk_cond, walk_body, st)

    def n_wide(jlo, jhi):
        return jnp.sum(jnp.where(jnp.logical_and(tie > 0.0, jhi - jlo > 1), 1.0, 0.0))

    def tie_cond(st):
        return jnp.logical_and(st[0] < 64, st[-1] > 0.0)

    def tie_body(st):
        it, jlo, jhi, c_lo, c_hi, _ = st
        span = (jhi - jlo).astype(F32)
        guess = jlo + jnp.floor(span * (rem - c_lo) / jnp.maximum(c_hi - c_lo, 1.0)).astype(I32)
        mid = lax.shift_right_arithmetic(jlo + jhi, 1)
        use_guess = jnp.logical_and(jnp.bitwise_and(it, 1) == 1, c_hi >= rem)
        jm = jnp.where(use_guess, guess, mid)
        jm = jnp.maximum(jnp.minimum(jm, jhi - 1), jlo + 1)
        c = count_tie_le(t_val, jm)
        wide = jhi - jlo > 1
        ge = jnp.logical_and(c >= rem, wide)
        lt = jnp.logical_and(c < rem, wide)
        jhi, c_hi = jnp.where(ge, jm, jhi), jnp.where(ge, c, c_hi)
        jlo, c_lo = jnp.where(lt, jm, jlo), jnp.where(lt, c, c_lo)
        jlo = jnp.where(jnp.logical_and(ge, c == rem), jm - 1, jlo)
        return it + 1, jlo, jhi, c_lo, c_hi, n_wide(jlo, jhi)

    jlo0, jhi0 = jnp.full(shape, -1, I32), jnp.full(shape, n_keys - 1, I32)
    st = (jnp.int32(0), jlo0, jhi0, zeros, zeros - 1.0, n_wide(jlo0, jhi0))
    _, _, jhi, _, _, _ = lax.while_loop(tie_cond, tie_body, st)
    j_val = jnp.where(tie > 0.0, jhi, j_val)
    return t_val, j_val


def _fold_rows(v, op):
    rows = v.shape[0]
    while rows > SUBLANES:
        rows //= 2
        v = op(v[0:rows], v[rows:2 * rows])
    return v


def _prompt_attn_kernel(qT_ref, qiT_ref, wiT_ref, k_ref, vT_ref, ki_ref, o_ref,
                        sc_ref, m_ref, l_ref, acc_ref, s_ref, *, tq, ch, cha, cat, seq):
    i = pl.program_id(0)
    n_ch = ((i + 1) * tq + ch - 1) // ch
    q_pos = i * tq + lax.broadcasted_iota(I32, (1, tq), 1)
    row_iota = lax.broadcasted_iota(I32, (ch, tq), 0)
    pair = MXU_WIDTH // tq

    w_idx = wiT_ref[...] * IDX_SCALE
    qi_rhs = [jnp.concatenate([qiT_ref[(g * pair + p) * IDX_DIM:(g * pair + p + 1) * IDX_DIM, :]
                               for p in range(pair)], axis=1) for g in range(IDX_HEADS // pair)]

    def score_body(c, amax):
        for r in range(ch // cha):
            start = pl.multiple_of(c * ch + r * cha, cha)
            kic = ki_ref[pl.ds(start, cha), :]
            acc = jnp.zeros((cha, tq), F32)
            for g in range(IDX_HEADS // pair):
                x = _dot(kic, qi_rhs[g])
                for p in range(pair):
                    h = g * pair + p
                    acc = acc + w_idx[h:h + 1, :] * jnp.maximum(x[:, p * tq:(p + 1) * tq], 0.0)
            amax = jnp.maximum(amax, _fold_rows(jnp.abs(acc), jnp.maximum))
            sc_ref[pl.ds(start, cha), :] = jnp.where(start + row_iota[0:cha] <= q_pos, acc, -jnp.inf)
        return amax

    amax = lax.fori_loop(0, n_ch, score_body, jnp.zeros((SUBLANES, tq), F32))
    amax = jnp.max(amax, axis=0, keepdims=True)

    def key_reduce(fn, init, combine, final):
        def body(c, acc):
            start = pl.multiple_of(c * ch, ch)
            return combine(acc, _fold_rows(fn(sc_ref[pl.ds(start, ch), :], start + row_iota), combine))
        return final(lax.fori_loop(0, n_ch, body, jnp.full((SUBLANES, tq), init, F32)), axis=0, keepdims=True)

    count = lambda fn: key_reduce(fn, 0.0, jnp.add, jnp.sum)
    lowest = lambda fn: key_reduce(fn, jnp.inf, jnp.minimum, jnp.min)
    count_ge = lambda t: count(lambda x, _: jnp.where(x >= t, 1.0, 0.0))
    min_ge = lambda t: lowest(lambda x, _: jnp.where(x >= t, x, jnp.inf))
    def count_gt_min_gt(t):
        def body(c, acc):
            x = sc_ref[pl.ds(pl.multiple_of(c * ch, ch), ch), :]
            gt = x > t
            return (acc[0] + _fold_rows(jnp.where(gt, 1.0, 0.0), jnp.add),
                    jnp.minimum(acc[1], _fold_rows(jnp.where(gt, x, jnp.inf), jnp.minimum)))
        init = (jnp.zeros((SUBLANES, tq), F32), jnp.full((SUBLANES, tq), jnp.inf, F32))
        cnt, low = lax.fori_loop(0, n_ch, body, init)
        return jnp.sum(cnt, axis=0, keepdims=True), jnp.min(low, axis=0, keepdims=True)
    count_tie_le = lambda t, jm: count(
        lambda x, kp: jnp.where(jnp.logical_and(x == t, kp <= jm), 1.0, 0.0))

    n_valid = (q_pos + 1).astype(F32)
    need = jnp.minimum(n_valid, float(min(TOP_K_MAX, seq // 4)))
    t_val, j_val = _select_topk(count_ge, min_ge, count_gt_min_gt, count_tie_le, amax, need, n_valid, seq)

    m_ref[...] = jnp.full(m_ref.shape, NEG_BIG, F32)
    l_ref[...] = jnp.zeros(l_ref.shape, F32)
    acc_ref[...] = jnp.zeros(acc_ref.shape, F32)
    zero_q = jnp.zeros((HEAD_DIM, tq), BF16)
    q_rhs = []
    for g in range(N_HEADS // pair):
        blocks = []
        for p in range(pair):
            qh = qT_ref[(g * pair + p) * HEAD_DIM:(g * pair + p + 1) * HEAD_DIM, :]
            blocks.append(jnp.concatenate([qh if pp == p else zero_q for pp in range(pair)], axis=1))
        q_rhs.append(jnp.concatenate(blocks, axis=0))
    ones_rows = jnp.ones((BF16_ROWS, cat), BF16)

    n_steps = n_ch * (ch // cat)

    def logits_pass(c, slot):
        start = pl.multiple_of(c * cat, cat)
        x = sc_ref[pl.ds(start, cat), :]
        sel = jnp.logical_or(x > t_val, jnp.logical_and(x == t_val, start + row_iota[0:cat] <= j_val))
        bias = jnp.where(sel, 0.0, -jnp.inf)
        m_cur = []
        for g in range(N_HEADS // pair):
            kc = k_ref[pl.ds(start, cat), g * pair * HEAD_DIM:(g + 1) * pair * HEAD_DIM]
            s_all = _dot(kc, q_rhs[g])
            for p in range(pair):
                s = s_all[:, p * tq:(p + 1) * tq] + bias
                s_ref[slot, g * pair + p] = s
                m_cur.append(jnp.max(_fold_rows(s, jnp.maximum), axis=0, keepdims=True))
        return jnp.concatenate(m_cur, axis=0)

    def softmax_pass(c, slot, m_cur):
        start = pl.multiple_of(c * cat, cat)
        m_prev = m_ref[...]
        m_new = jnp.maximum(m_prev, m_cur)
        alpha = jnp.exp2(m_prev - m_new)
        m_ref[...] = m_new
        l_cur = []
        for h in range(N_HEADS):
            pr = jnp.exp2(s_ref[slot, h] - m_new[h:h + 1, :]).astype(BF16)
            lhs = jnp.concatenate([vT_ref[h * HEAD_DIM:(h + 1) * HEAD_DIM, pl.ds(start, cat)], ones_rows], axis=0)
            pv = _dot(lhs, pr)
            rows = pl.ds(h * HEAD_DIM, HEAD_DIM)
            acc_ref[rows, :] = alpha[h:h + 1, :] * acc_ref[rows, :] + pv[0:HEAD_DIM]
            l_cur.append(pv[HEAD_DIM:HEAD_DIM + 1])
        l_ref[...] = alpha * l_ref[...] + jnp.concatenate(l_cur, axis=0)

    def attn_body(cc, m_cur):
        c0 = 2 * cc
        m_1 = logits_pass(c0 + 1, 1)
        softmax_pass(c0, 0, m_cur)
        m_2 = logits_pass(jnp.minimum(c0 + 2, n_steps - 1), 0)
        softmax_pass(c0 + 1, 1, m_1)
        return m_2

    lax.fori_loop(0, n_steps // 2, attn_body, logits_pass(0, 0))
    outs =[acc_ref[h * HEAD_DIM:(h + 1) * HEAD_DIM, :] / l_ref[h:h + 1, :] for h in range(N_HEADS)]
    o_ref[...] = jnp.concatenate(outs, axis=0).T


def _prompt_attention(qT, qiT, wiT, k, vT, ki, tq, ch):
    seq = qT.shape[1]
    cha = min(ch, MXU_WIDTH)
    cat = ch // 2
    kern = functools.partial(_prompt_attn_kernel, tq=tq, ch=ch, cha=cha, cat=cat, seq=seq)
    col = lambda r: pl.BlockSpec((r, tq), lambda i: (0, i))
    return pl.pallas_call(
        kern, grid=(seq // tq,),
        in_specs=[col(ATT_WIDTH), col(IDX_WIDTH), col(IDX_HEADS),
                  _const_spec(k.shape), _const_spec(vT.shape), _const_spec(ki.shape)],
        out_specs=pl.BlockSpec((tq, ATT_WIDTH), lambda i: (i, 0)),
        out_shape=jax.ShapeDtypeStruct((seq, ATT_WIDTH), F32),
        scratch_shapes=[pltpu.VMEM((seq, tq), F32),
                        pltpu.VMEM((N_HEADS, tq), F32),
                        pltpu.VMEM((N_HEADS, tq), F32),
                        pltpu.VMEM((ATT_WIDTH, tq), F32),
                        pltpu.VMEM((2, N_HEADS, cat, tq), F32)],
        compiler_params=_params(("arbitrary",)), name="prompt_attention",
    )(qT, qiT, wiT, k, vT, ki)


def _lru_gates(xc, wrg_ref, brg_ref, wig_ref, big_ref, lam_ref):
    xcb = xc.astype(BF16)
    r = _sigmoid(_dot(xcb, wrg_ref[...]) + brg_ref[...])
    ig = _sigmoid(_dot(xcb, wig_ref[...]) + big_ref[...])
    nlam = -lam_ref[...]
    softplus = jnp.maximum(nlam, 0.0) + _log1p(jnp.exp(-jnp.abs(nlam)))
    log_a = (-LRU_C) * r * softplus
    a = jnp.exp(log_a)
    u = jnp.sqrt(-_expm1(2.0 * log_a)) * (ig * xc)
    return a, u


def _merge(x, ya, g_att, yl, g_lru, ga, gb, woa_ref, wob_ref, wout_ref):
    pa = _dot((ya * _silu(g_att)).astype(BF16), woa_ref[...])
    pb = _dot((yl * _silu(g_lru)).astype(BF16), wob_ref[...])
    m = _sigmoid(ga) * pa + _sigmoid(gb) * pb
    return x + _dot(m.astype(BF16), wout_ref[...])


def _prompt_lru_kernel(x_ref, g_ref, wlru_ref, cw_ref, cb_ref, wrg_ref, brg_ref, wig_ref, big_ref, lam_ref,
                       ya_ref, gatt_ref, woa_ref, wob_ref, wout_ref,
                       xo_ref, tail_ref, hlast_ref, xl_sc, h_sc, *, tm, width):
    i = pl.program_id(0)

    @pl.when(i == 0)
    def _():
        xl_sc[0:SUBLANES, :] = jnp.zeros((SUBLANES, width), F32)
        h_sc[...] = jnp.zeros(h_sc.shape, F32)

    x = x_ref[...]
    hb = _rmsnorm(x, g_ref[...]).astype(BF16)
    z = _dot(hb, wlru_ref[...])
    xl_sc[SUBLANES:SUBLANES + tm, :] = z[:, 0:width]
    xc = cb_ref[...]
    for j in range(CONV_WIDTH):
        xc = xc + cw_ref[j:j + 1, :] * xl_sc[pl.ds(SUBLANES - (CONV_WIDTH - 1) + j, tm), :]
    a, b = _lru_gates(xc, wrg_ref, brg_ref, wig_ref, big_ref, lam_ref)

    t_idx = lax.broadcasted_iota(I32, (tm, width), 0)
    d = 1
    while d < tm:
        a_s = pltpu.roll(a, d, 0)
        b_s = pltpu.roll(b, d, 0)
        live = t_idx >= d
        b = jnp.where(live, a * b_s + b, b)
        a = jnp.where(live, a * a_s, a)
        d *= 2
    hs = a * h_sc[0:1, :] + b
    h_sc[...] = jnp.broadcast_to(hs[tm - 1:tm, :], h_sc.shape)
    hlast_ref[...] = h_sc[...]
    tail_ref[...] = xl_sc[tm:tm + SUBLANES, :]
    xl_sc[0:SUBLANES, :] = xl_sc[tm:tm + SUBLANES, :]

    xo_ref[...] = _merge(x, ya_ref[...], gatt_ref[...], hs, z[:, width:2 * width],
                         z[:, 2 * width:3 * width], z[:, 3 * width:4 * width], woa_ref, wob_ref, wout_ref)


def _prompt_lru_merge(x, ya, gatt, w, tm):
    m, d = x.shape
    width = w["rg"].shape[0]
    kern = functools.partial(_prompt_lru_kernel, tm=tm, width=width)
    row = lambda c: pl.BlockSpec((tm, c), lambda i: (i, 0))
    consts = [w["g"], w["lru"], w["conv_w"], w["conv_b"], w["rg"], w["b_rg"], w["ig"], w["b_ig"], w["lam"]]
    consts2 = [w["oa"], w["ob"], w["out"]]
    state = pl.BlockSpec((SUBLANES, width), lambda i: (0, 0))
    return pl.pallas_call(
        kern, grid=(m // tm,),
        in_specs=[row(d)] + [_const_spec(c.shape) for c in consts] + [row(ATT_WIDTH), row(ATT_WIDTH)]
                 + [_const_spec(c.shape) for c in consts2],
        out_specs=(row(d), state, state),
        out_shape=(jax.ShapeDtypeStruct((m, d), F32), jax.ShapeDtypeStruct((SUBLANES, width), F32),
                   jax.ShapeDtypeStruct((SUBLANES, width), F32)),
        scratch_shapes=[pltpu.VMEM((tm + SUBLANES, width), F32), pltpu.VMEM((SUBLANES, width), F32)],
        compiler_params=_params(("arbitrary",)), name="prompt_lru_merge",
    )(x, *consts, ya, gatt, *consts2)


def _sample_lru_kernel(x_ref, g_ref, wlru_ref, cw_ref, cb_ref, wrg_ref, brg_ref, wig_ref, big_ref, lam_ref,
                       c0_ref, c1_ref, c2_ref, h0_ref, ya_ref, gatt_ref, woa_ref, wob_ref, wout_ref,
                       xo_ref, xl_ref, h_ref, *, width):
    x = x_ref[...]
    hb = _rmsnorm(x, g_ref[...]).astype(BF16)
    z = _dot(hb, wlru_ref[...])
    xl = z[:, 0:width]
    xc = cb_ref[...]
    for j, past in enumerate((c0_ref, c1_ref, c2_ref)):
        xc = xc + cw_ref[j:j + 1, :] * past[...]
    xc = xc + cw_ref[CONV_WIDTH - 1:CONV_WIDTH, :] * xl
    a, u = _lru_gates(xc, wrg_ref, brg_ref, wig_ref, big_ref, lam_ref)
    h = u + a * h0_ref[...]
    xl_ref[...] = xl
    h_ref[...] = h
    xo_ref[...] = _merge(x, ya_ref[...], gatt_ref[...], h, z[:, width:2 * width],
                         z[:, 2 * width:3 * width], z[:, 3 * width:4 * width], woa_ref, wob_ref, wout_ref)


def _sample_lru_merge(x, conv_state, h0, ya, gatt, w):
    b, d = x.shape
    width = w["rg"].shape[0]
    kern = functools.partial(_sample_lru_kernel, width=width)
    args = [x, w["g"], w["lru"], w["conv_w"], w["conv_b"], w["rg"], w["b_rg"], w["ig"], w["b_ig"], w["lam"],
            conv_state[:, 0], conv_state[:, 1], conv_state[:, 2], h0, ya, gatt, w["oa"], w["ob"], w["out"]]
    full = lambda a: pl.BlockSpec(a.shape, lambda i: (0,) * a.ndim)
    out_shape = (jax.ShapeDtypeStruct((b, d), F32), jax.ShapeDtypeStruct((b, width), F32),
                 jax.ShapeDtypeStruct((b, width), F32))
    return pl.pallas_call(
        kern, grid=(1,), in_specs=[full(a) for a in args],
        out_specs=tuple(pl.BlockSpec(s.shape, lambda i: (0, 0)) for s in out_shape), out_shape=out_shape,
        compiler_params=_params(("arbitrary",)), name="sample_lru_merge",
    )(*args)


def _sample_score_kernel(pt_ref, qi_ref, wi_ref, kin_ref, cki_ref, o_ref, kbuf, sem,
                         *, layer_page0, n_pages, page, ch):
    b = pl.program_id(0)

    def page_copy(p):
        phys = pt_ref[b, p]
        return pltpu.make_async_copy(cki_ref.at[layer_page0 + phys],
                                     kbuf.at[:, pl.ds(pl.multiple_of(p * page, page), page)], sem)

    def start(p, c):
        page_copy(p).start()
        return c

    def wait(p, c):
        page_copy(p).wait()
        return c

    lax.fori_loop(0, n_pages, start, 0)
    lax.fori_loop(0, n_pages, wait, 0)

    qi = qi_ref[0]
    w = wi_ref[0] * IDX_SCALE
    n_sub = ch // LANES

    def body(c, carry):
        start_col = pl.multiple_of(c * ch, ch)
        x = _dot(qi, kbuf[:, pl.ds(start_col, ch)].astype(BF16))
        s = jnp.sum(w * jnp.maximum(x, 0.0), axis=0, keepdims=True)
        for j in range(n_sub):
            o_ref[0, pl.ds(c * n_sub + j, 1), :] = s[:, j * LANES:(j + 1) * LANES]
        return carry

    lax.fori_loop(0, (n_pages * page) // ch, body, 0)

    kn = kin_ref[0].astype(BF16).astype(F32)
    xn = jnp.sum(qi.astype(F32) * kn, axis=-1, keepdims=True)
    sn = jnp.sum(w * jnp.maximum(xn, 0.0), axis=0, keepdims=True)
    n_rows = (n_pages * page) // LANES
    tail_rows = o_ref.shape[1] - n_rows
    r_i = lax.broadcasted_iota(I32, (tail_rows, LANES), 0)
    l_i = lax.broadcasted_iota(I32, (tail_rows, LANES), 1)
    o_ref[0, n_rows:n_rows + tail_rows, :] = jnp.where(jnp.logical_and(r_i == 0, l_i == 0), sn, -jnp.inf)


def _sample_scores(page_table, qi, wi, ki_new, ckiT, layer, n_pool, ch):
    b, n_pages = page_table.shape
    page = ckiT.shape[2]
    n_rows = pl.cdiv((n_pages * page) // LANES + 1, SUBLANES) * SUBLANES
    kern = functools.partial(_sample_score_kernel, layer_page0=layer * n_pool, n_pages=n_pages, page=page, ch=ch)
    grid_spec = pltpu.PrefetchScalarGridSpec(
        num_scalar_prefetch=1, grid=(b,),
        in_specs=[pl.BlockSpec((1, IDX_HEADS, IDX_DIM), lambda i, pt: (i, 0, 0)),
                  pl.BlockSpec((1, IDX_HEADS, 1), lambda i, pt: (i, 0, 0)),
                  pl.BlockSpec((1, 1, IDX_DIM), lambda i, pt: (i, 0, 0)),
                  pl.BlockSpec(memory_space=pl.ANY)],
        out_specs=pl.BlockSpec((1, n_rows, LANES), lambda i, pt: (i, 0, 0)),
        scratch_shapes=[pltpu.VMEM((IDX_DIM, n_pages * page), F32), pltpu.SemaphoreType.DMA(())],
    )
    return pl.pallas_call(
        kern, grid_spec=grid_spec, out_shape=jax.ShapeDtypeStruct((b, n_rows, LANES), F32),
        compiler_params=_params(("arbitrary",)), name="sample_scores",
    )(page_table, qi, wi, ki_new, ckiT)


def _sample_select_kernel(sc_ref, bias_ref, *, n_keys, k_top):
    x = sc_ref[...]
    nb = x.shape[0]
    kp = (lax.broadcasted_iota(I32, x.shape, 1) * LANES + lax.broadcasted_iota(I32, x.shape, 2))
    red = lambda a, op: op(op(a, axis=2, keepdims=True), axis=1, keepdims=True)
    amax = red(jnp.where(x == -jnp.inf, 0.0, jnp.abs(x)), jnp.max)
    count_ge = lambda t: red(jnp.where(x >= t, 1.0, 0.0), jnp.sum)
    min_ge = lambda t: red(jnp.where(x >= t, x, jnp.inf), jnp.min)
    count_gt_min_gt = lambda t: (red(jnp.where(x > t, 1.0, 0.0), jnp.sum),
                                 red(jnp.where(x > t, x, jnp.inf), jnp.min))
    count_tie_le = lambda t, jm: red(jnp.where(jnp.logical_and(x == t, kp <= jm), 1.0, 0.0), jnp.sum)
    need = jnp.full((nb, 1, 1), float(k_top), F32)
    n_valid = jnp.full((nb, 1, 1), float(n_keys), F32)
    t_val, j_val = _select_topk(count_ge, min_ge, count_gt_min_gt, count_tie_le, amax, need, n_valid, n_keys)
    sel = jnp.logical_or(x > t_val, jnp.logical_and(x == t_val, kp <= j_val))
    bias_ref[...] = jnp.where(sel, 0.0, -jnp.inf)


def _sample_select(scores, n_keys, k_top):
    kern = functools.partial(_sample_select_kernel, n_keys=n_keys, k_top=k_top)
    full = pl.BlockSpec(scores.shape, lambda i: (0, 0, 0))
    return pl.pallas_call(
        kern, grid=(1,), in_specs=[full], out_specs=full,
        out_shape=jax.ShapeDtypeStruct(scores.shape, F32),
        compiler_params=_params(("arbitrary",)), name="sample_select",
    )(scores)


def _sample_attn_kernel(pt_ref, bias_ref, q_ref, kn_ref, vn_ref, ck_ref, cv_ref, o_ref,
                        kbuf, vbuf, acc_sc, sem, *, layer_page0, n_pages, page, group):
    b = pl.program_id(0)
    n_groups = n_pages // group

    def copies(g, slot):
        out = []
        for j in range(group):
            phys = pt_ref[b, g * group + j]
            out.append(pltpu.make_async_copy(ck_ref.at[layer_page0 + phys], kbuf.at[slot, j], sem.at[0, slot]))
            out.append(pltpu.make_async_copy(cv_ref.at[layer_page0 + phys], vbuf.at[slot, j], sem.at[1, slot]))
        return out

    def start(g, slot):
        for cp in copies(g, slot):
            cp.start()

    def wait(g, slot):
        for cp in copies(g, slot):
            cp.wait()

    q = q_ref[0]
    qb = jnp.broadcast_to(q, (N_HEADS, HEAD_DIM, page))
    acc_sc[...] = jnp.zeros(acc_sc.shape, F32)
    start(0, 0)

    def body(g, carry):
        m_prev, l_prev = carry
        slot = g % 2
        wait(g, slot)

        @pl.when(g + 1 < n_groups)
        def _():
            start(g + 1, 1 - slot)

        s = []
        for j in range(group):
            kt = kbuf[slot, j]
            sj = jnp.sum(kt * qb, axis=1, keepdims=True)
            s.append(sj + bias_ref[0, pl.ds(g * group + j, 1), :][None])
        m_cur = s[0]
        for sj in s[1:]:
            m_cur = jnp.maximum(m_cur, sj)
        m_new = jnp.maximum(m_prev, jnp.max(m_cur, axis=2, keepdims=True))
        alpha = jnp.exp2(m_prev - m_new)
        pv = jnp.zeros((N_HEADS, HEAD_DIM, page), F32)
        l_cur = jnp.zeros((N_HEADS, 1, page), F32)
        for j in range(group):
            pj = jnp.exp2(s[j] - m_new)
            l_cur = l_cur + pj
            pv = pv + vbuf[slot, j] * pj
        acc_sc[...] = alpha * acc_sc[...] + pv
        return m_new, alpha * l_prev + jnp.sum(l_cur, axis=2, keepdims=True)

    init = (jnp.full((N_HEADS, 1, 1), NEG_BIG, F32), jnp.zeros((N_HEADS, 1, 1), F32))
    m_run, l_run = lax.fori_loop(0, n_groups, body, init)

    s_new = (jnp.sum(kn_ref[0] * q, axis=1, keepdims=True)
             + bias_ref[0, n_pages:n_pages + 1, 0:1][None])
    m_fin = jnp.maximum(m_run, s_new)
    alpha = jnp.exp2(m_run - m_fin)
    p_new = jnp.exp2(s_new - m_fin)
    l_fin = alpha * l_run + p_new
    out = (alpha * jnp.sum(acc_sc[...], axis=2, keepdims=True) + p_new * vn_ref[0]) / l_fin
    o_ref[0] = out


def _sample_attention(bias, page_table, q, k_new, v_new, ckT, cvT, layer, n_pool, group):
    b, n_pages = page_table.shape
    page = ckT.shape[-1]
    kern = functools.partial(_sample_attn_kernel, layer_page0=layer * n_pool, n_pages=n_pages, page=page,
                             group=group)
    per_b = pl.BlockSpec((1, N_HEADS, HEAD_DIM, 1), lambda i, p: (i, 0, 0, 0))
    grid_spec = pltpu.PrefetchScalarGridSpec(
        num_scalar_prefetch=1, grid=(b,),
        in_specs=[pl.BlockSpec((1,) + bias.shape[1:], lambda i, p: (i, 0, 0)),
                  per_b, per_b, per_b,
                  pl.BlockSpec(memory_space=pl.ANY), pl.BlockSpec(memory_space=pl.ANY)],
        out_specs=per_b,
        scratch_shapes=[pltpu.VMEM((2, group, N_HEADS, HEAD_DIM, page), F32),
                        pltpu.VMEM((2, group, N_HEADS, HEAD_DIM, page), F32),
                        pltpu.VMEM((N_HEADS, HEAD_DIM, page), F32),
                        pltpu.SemaphoreType.DMA((2, 2))],
    )
    col = lambda a: a.reshape(b, N_HEADS, HEAD_DIM, 1)
    out = pl.pallas_call(
        kern, grid_spec=grid_spec, out_shape=jax.ShapeDtypeStruct((b, N_HEADS, HEAD_DIM, 1), F32),
        compiler_params=_params(("arbitrary",)), name="sample_attention",
    )(page_table, bias, col(q), col(k_new), col(v_new), ckT, cvT)
    return out.reshape(b, ATT_WIDTH)


def _norm_kernel(x_ref, g_ref, o_ref):
    o_ref[...] = _rmsnorm(x_ref[...], g_ref[...])


def _final_norm(x, g, tm):
    m, d = x.shape
    return pl.pallas_call(
        _norm_kernel, grid=(m // tm,),
        in_specs=[pl.BlockSpec((tm, d), lambda i: (i, 0)), pl.BlockSpec((1, d), lambda i: (0, 0))],
        out_specs=pl.BlockSpec((tm, d), lambda i: (i, 0)), out_shape=jax.ShapeDtypeStruct((m, d), F32),
        compiler_params=_params(("arbitrary",)), name="final_norm",
    )(x, g)


def _layer_weights(l, norm_g, w_in, conv_w, conv_b, w_rg, b_rg, w_ig, b_ig, lru_lam, w_oa, w_ob, w_out):
    d = w_in.shape[1]
    width = lru_lam.shape[1]
    wl = w_in[l]
    o_g = 3 * ATT_WIDTH
    o_qi = 4 * ATT_WIDTH
    o_ki = o_qi + IDX_WIDTH
    o_wi = o_ki + IDX_DIM
    o_lru = o_wi + IDX_HEADS
    att_t = jnp.concatenate([wl[:, 0:o_g], wl[:, o_qi:o_lru]], axis=1).T
    att_t = jnp.pad(att_t, ((0, (-att_t.shape[0]) % BF16_ROWS), (0, 0)))
    blockdiag = lambda w: jax.scipy.linalg.block_diag(*[w[n] for n in range(w.shape[0])])
    r2 = lambda v: v.reshape(1, -1)
    return {
        "g": r2(norm_g[l]),
        "attT": att_t.astype(BF16),
        "k": wl[:, ATT_WIDTH:2 * ATT_WIDTH].astype(BF16),
        "gatt": wl[:, o_g:o_qi].astype(BF16),
        "ki": wl[:, o_ki:o_wi].astype(BF16),
        "lru": wl[:, o_lru:o_lru + 2 * width + 2 * d].astype(BF16),
        "conv_w": conv_w[l], "conv_b": r2(conv_b[l]),
        "rg": blockdiag(w_rg[l]).astype(BF16), "b_rg": r2(b_rg[l]),
        "ig": blockdiag(w_ig[l]).astype(BF16), "b_ig": r2(b_ig[l]),
        "lam": r2(lru_lam[l]),
        "oa": w_oa[l].astype(BF16), "ob": w_ob[l].astype(BF16), "out": w_out[l].astype(BF16),
    }


def _pick_tile(m, target):
    t = min(m, target)
    while m % t:
        t //= 2
    return t


def kernel(x_prompt, x_sample, cache_k, cache_v, cache_kidx, state_conv, state_h, page_table,
           norm_g, w_in, conv_w, conv_b, w_rg, b_rg, w_ig, b_ig, lru_lam, w_oa, w_ob, w_out, final_g):
    bp, seq, d = x_prompt.shape
    bs, dec_seq, _ = x_sample.shape
    depth, n_pool, page = cache_k.shape[0], cache_k.shape[1], cache_k.shape[2]
    assert bp == 1 and dec_seq == 1, "one prompt sequence and one new token per sampled sequence"
    assert page == LANES, "a cache page fills the lane axis"
    n_pages = page_table.shape[1]
    past = n_pages * page
    k_top_s = min(TOP_K_MAX, (past + dec_seq) // 4)

    ckT = jnp.transpose(cache_k, (0, 1, 3, 4, 2)).reshape(depth * n_pool, N_HEADS, HEAD_DIM, page)
    cvT = jnp.transpose(cache_v, (0, 1, 3, 4, 2)).reshape(depth * n_pool, N_HEADS, HEAD_DIM, page)
    ckiT = jnp.transpose(cache_kidx, (0, 1, 3, 2)).reshape(depth * n_pool, IDX_DIM, page)

    tm_proj = _pick_tile(seq, 256)
    tq = _pick_tile(seq, 128)
    ch = _pick_tile(seq, 512)
    ch_s = _pick_tile(past, 1024)
    group = _pick_tile(n_pages, 8)

    xp = x_prompt.reshape(seq, d)
    xs = x_sample.reshape(bs, d)
    outs = {n: [] for n in ("pk", "pv", "pki", "pconv", "ph", "sk", "sv", "ski", "sconv", "sh")}
    heads_last = lambda t, n: jnp.transpose(t.reshape(N_HEADS, HEAD_DIM, n), (2, 0, 1))
    for l in range(depth):
        w = _layer_weights(l, norm_g, w_in, conv_w, conv_b, w_rg, b_rg, w_ig, b_ig, lru_lam, w_oa, w_ob, w_out)
        qT, kT, vT, qiT, kiT, wiT, _, vTb, kb, kib, gatt = _proj(xp, w["g"], w, tm_proj)
        ya = _prompt_attention(qT, qiT, wiT, kb, vTb, kib, tq, ch)
        xp, tail, hlast = _prompt_lru_merge(xp, ya, gatt, w, tm_proj)
        outs["pk"].append(heads_last(kT, seq)[None])
        outs["pv"].append(heads_last(vT, seq)[None])
        outs["pki"].append(kiT.T[None])
        outs["pconv"].append(tail[SUBLANES - (CONV_WIDTH - 1):].reshape(1, CONV_WIDTH - 1, -1))
        outs["ph"].append(hlast[0:1])
        qT, kT, vT, qiT, kiT, wiT, _, _, _, _, gatt = _proj(xs, w["g"], w, bs)
        k_new, v_new, ki_new = kT.T, vT.T, kiT.T
        scores = _sample_scores(page_table, qiT.T.reshape(bs, IDX_HEADS, IDX_DIM), wiT.T.reshape(bs, IDX_HEADS, 1),
                                ki_new.reshape(bs, 1, IDX_DIM), ckiT, l, n_pool, ch_s)
        bias = _sample_select(scores, past + dec_seq, k_top_s)
        ya = _sample_attention(bias, page_table, qT.T.astype(F32), k_new, v_new, ckT, cvT, l, n_pool, group)
        xs, xl, h = _sample_lru_merge(xs, state_conv[l], state_h[l], ya, gatt, w)
        outs["sk"].append(k_new.reshape(bs, 1, N_HEADS, HEAD_DIM))
        outs["sv"].append(v_new.reshape(bs, 1, N_HEADS, HEAD_DIM))
        outs["ski"].append(ki_new.reshape(bs, 1, IDX_DIM))
        outs["sconv"].append(jnp.concatenate([state_conv[l][:, 1:], xl[:, None, :]], axis=1))
        outs["sh"].append(h)
    y_prompt = _final_norm(xp, final_g.reshape(1, d), tm_proj).reshape(1, seq, d)
    y_sample = _final_norm(xs, final_g.reshape(1, d), bs).reshape(bs, 1, d)
    st = lambda n: jnp.stack(outs[n])
    return (y_prompt, y_sample, st("pk"), st("pv"), st("pki"), st("pconv"), st("ph"),
            st("sk"), st("sv"), st("ski"), st("sconv"), st("sh"))
```

```python
import functools

import jax
import jax.numpy as jnp
from jax import lax
from jax.experimental import pallas as pl
from jax.experimental.pallas import tpu as pltpu

F32 = jnp.float32
BF16 = jnp.bfloat16
I32 = jnp.int32

N_HEADS = 8
HEAD_DIM = 64
ATT_WIDTH = N_HEADS * HEAD_DIM
IDX_HEADS = 8
IDX_DIM = 64
IDX_WIDTH = IDX_HEADS * IDX_DIM
TOP_K_MAX = 256
CONV_WIDTH = 4
LRU_C = 8.0
NORM_EPS = 1e-6
IDX_SCALE = (IDX_DIM ** -0.5) * (IDX_HEADS ** -0.5)
ATT_SCALE = HEAD_DIM ** -0.5
LOG2E = 1.4426950408889634
NEG_BIG = -1e30
LANES = 128
SUBLANES = 8
BF16_ROWS = 16
MXU_WIDTH = 256
VMEM_LIMIT_BYTES = 56 * 1024 * 1024

_NT = (((1,), (1,)), ((), ()))


def _dot(a, b):
    return jnp.dot(a, b, preferred_element_type=F32)


def _dot_nt(a, b):
    return lax.dot_general(a, b, _NT, preferred_element_type=F32)


def _rmsnorm(x, g):
    return (x * lax.rsqrt(jnp.mean(x * x, axis=-1, keepdims=True) + NORM_EPS)) * g


def _sigmoid(x):
    return 1.0 / (1.0 + jnp.exp(-x))


def _silu(x):
    return x * _sigmoid(x)


def _log1p(e):
    w = 1.0 + e
    return jnp.where(w == 1.0, e, jnp.log(w) * (e / (w - 1.0)))


def _expm1(x):
    u = jnp.exp(x)
    um1 = u - 1.0
    return jnp.where(um1 == 0.0, x, jnp.where(u == 0.0, -1.0, um1 * (x / jnp.log(u))))


def _const_spec(shape):
    nd = len(shape)
    return pl.BlockSpec(shape, lambda *_: (0,) * nd, pipeline_mode=pl.Buffered(1))


def _params(sem):
    return pltpu.CompilerParams(dimension_semantics=sem, vmem_limit_bytes=VMEM_LIMIT_BYTES)


def _proj_kernel(x_ref, g_ref, wT_ref, wk_ref, wg_ref, wki_ref,
                 qT_ref, kT_ref, vT_ref, qiT_ref, kiT_ref, wiT_ref, kTb_ref, vTb_ref, k_ref, ki_ref, gatt_ref):
    hb = _rmsnorm(x_ref[...], g_ref[...]).astype(BF16)
    zT = _dot_nt(wT_ref[...], hb)
    o_k, o_v, o_qi = ATT_WIDTH, 2 * ATT_WIDTH, 3 * ATT_WIDTH
    o_ki = o_qi + IDX_WIDTH
    o_wi = o_ki + IDX_DIM
    qT_ref[...] = (zT[0:o_k] * (ATT_SCALE * LOG2E)).astype(BF16)
    kT_ref[...] = zT[o_k:o_v]
    vT_ref[...] = zT[o_v:o_qi]
    kTb_ref[...] = zT[o_k:o_v].astype(BF16)
    vTb_ref[...] = zT[o_v:o_qi].astype(BF16)
    qiT_ref[...] = zT[o_qi:o_ki].astype(BF16)
    kiT_ref[...] = zT[o_ki:o_wi]
    wiT_ref[...] = zT[o_wi:o_wi + IDX_HEADS]
    k_ref[...] = _dot(hb, wk_ref[...]).astype(BF16)
    ki_ref[...] = _dot(hb, wki_ref[...]).astype(BF16)
    gatt_ref[...] = _dot(hb, wg_ref[...])


def _proj(x, g, w, tm):
    m, d = x.shape
    row = lambda c: pl.BlockSpec((tm, c), lambda i: (i, 0))
    col = lambda r: pl.BlockSpec((r, tm), lambda i: (0, i))
    outs = (
        ((ATT_WIDTH, m), BF16, col),
        ((ATT_WIDTH, m), F32, col),
        ((ATT_WIDTH, m), F32, col),
        ((IDX_WIDTH, m), BF16, col),
        ((IDX_DIM, m), F32, col),
        ((IDX_HEADS, m), F32, col),
        ((ATT_WIDTH, m), BF16, col),
        ((ATT_WIDTH, m), BF16, col),
        ((m, ATT_WIDTH), BF16, row),
        ((m, IDX_DIM), BF16, row),
        ((m, ATT_WIDTH), F32, row),
    )
    out_shape = tuple(jax.ShapeDtypeStruct(s, t) for s, t, _ in outs)
    out_specs = tuple(f(s[0]) if f is col else f(s[1]) for s, _, f in outs)
    consts = [g, w["attT"], w["k"], w["gatt"], w["ki"]]
    return pl.pallas_call(
        _proj_kernel, grid=(m // tm,), in_specs=[row(d)] + [_const_spec(c.shape) for c in consts],
        out_specs=out_specs, out_shape=out_shape,
        compiler_params=_params(("arbitrary",)), name="proj",
    )(x, *consts)


MAX_BISECT = 24
PARK_SPAN = 8.0


def _select_topk(count_ge, min_ge, count_gt_min_gt, count_tie_le, amax, need, n_valid, n_keys):
    shape = need.shape
    big_j = jnp.full(shape, n_keys, I32)
    hi0 = 2.0 * amax + 1e-30
    lo0 = -hi0

    def n_open(done):
        return jnp.sum(1.0 - done)

    def bis_cond(st):
        return jnp.logical_and(st[0] < MAX_BISECT, st[-1] > 0.0)

    def bis_body(st):
        it, lo, hi, c_lo, c_hi, stale, thr, done, parked, _ = st
        probe = lo + 0.5 * (hi - lo)
        c = count_ge(probe)
        active = jnp.logical_and(done == 0.0, parked == 0.0)
        hit = jnp.logical_and(c == need, active)
        thr = jnp.where(hit, probe, thr)
        done = jnp.where(hit, 1.0, done)
        stale = jnp.where(jnp.logical_or(c == c_lo, c == c_hi), stale + 1.0, 0.0)
        up = jnp.logical_and(c > need, active)
        down = jnp.logical_and(c < need, active)
        lo, c_lo = jnp.where(up, probe, lo), jnp.where(up, c, c_lo)
        hi, c_hi = jnp.where(down, probe, hi), jnp.where(down, c, c_hi)
        give_up = jnp.logical_and(stale >= 2.0, c_lo - c_hi <= PARK_SPAN)
        parked = jnp.where(jnp.logical_and(give_up, done == 0.0), 1.0, parked)
        return it + 1, lo, hi, c_lo, c_hi, stale, thr, done, parked, jnp.sum((1.0 - done) * (1.0 - parked))

    zeros = jnp.zeros(shape, F32)
    st = (jnp.int32(0), lo0, hi0, n_valid, zeros, zeros, zeros, zeros, zeros, jnp.float32(1.0))
    _, lo, _, _, _, _, thr, done, _, _ = lax.while_loop(bis_cond, lambda s: bis_body(bis_body(s)), st)

    def walk_cond(st):
        return st[-1] > 0.0

    def walk_body(st):
        lo, t_val, j_val, rem, done, tie, _ = st
        t_cand = min_ge(lo)
        c_gt, next_val = count_gt_min_gt(t_cand)
        fin = jnp.logical_and(c_gt <= need, done == 0.0)
        t_val = jnp.where(fin, t_cand, t_val)
        has_tie = jnp.logical_and(fin, c_gt < need)
        j_val = jnp.where(jnp.logical_and(fin, c_gt == need), -1, j_val)
        rem = jnp.where(has_tie, need - c_gt, rem)
        tie = jnp.where(has_tie, 1.0, tie)
        done = jnp.where(fin, 1.0, done)
        lo = jnp.where(done == 0.0, next_val, lo)
        return lo, t_val, j_val, rem, done, tie, n_open(done)

    st = (lo, thr, big_j, zeros, done, zeros, n_open(done))
    _, t_val, j_val, rem, _, tie, _ = lax.while_loop(walk_cond, walk_body, st)

    def n_wide(jlo, jhi):
        return jnp.sum(jnp.where(jnp.logical_and(tie > 0.0, jhi - jlo > 1), 1.0, 0.0))

    def tie_cond(st):
        return jnp.logical_and(st[0] < 64, st[-1] > 0.0)

    def tie_body(st):
        it, jlo, jhi, c_lo, c_hi, _ = st
        span = (jhi - jlo).astype(F32)
        guess = jlo + jnp.floor(span * (rem - c_lo) / jnp.maximum(c_hi - c_lo, 1.0)).astype(I32)
        mid = lax.shift_right_arithmetic(jlo + jhi, 1)
        use_guess = jnp.logical_and(jnp.bitwise_and(it, 1) == 1, c_hi >= rem)
        jm = jnp.where(use_guess, guess, mid)
        jm = jnp.maximum(jnp.minimum(jm, jhi - 1), jlo + 1)
        c = count_tie_le(t_val, jm)
        wide = jhi - jlo > 1
        ge = jnp.logical_and(c >= rem, wide)
        lt = jnp.logical_and(c < rem, wide)
        jhi, c_hi = jnp.where(ge, jm, jhi), jnp.where(ge, c, c_hi)
        jlo, c_lo = jnp.where(lt, jm, jlo), jnp.where(lt, c, c_lo)
        jlo = jnp.where(jnp.logical_and(ge, c == rem), jm - 1, jlo)
        return it + 1, jlo, jhi, c_lo, c_hi, n_wide(jlo, jhi)

    jlo0, jhi0 = jnp.full(shape, -1, I32), jnp.full(shape, n_keys - 1, I32)
    st = (jnp.int32(0), jlo0, jhi0, zeros, zeros - 1.0, n_wide(jlo0, jhi0))
    _, _, jhi, _, _, _ = lax.while_loop(tie_cond, tie_body, st)
    j_val = jnp.where(tie > 0.0, jhi, j_val)
    return t_val, j_val


def _fold_rows(v, op):
    rows = v.shape[0]
    while rows > SUBLANES:
        rows //= 2
        v = op(v[0:rows], v[rows:2 * rows])
    return v


def _prompt_attn_kernel(qT_ref, qiT_ref, wiT_ref, k_ref, vT_ref, ki_ref, o_ref,
                        sc_ref, m_ref, l_ref, acc_ref, s_ref, *, tq, ch, cha, cat, seq):
    i = pl.program_id(0)
    n_ch = ((i + 1) * tq + ch - 1) // ch
    q_pos = i * tq + lax.broadcasted_iota(I32, (1, tq), 1)
    row_iota = lax.broadcasted_iota(I32, (ch, tq), 0)
    pair = MXU_WIDTH // tq

    w_idx = wiT_ref[...] * IDX_SCALE
    qi_rhs = [jnp.concatenate([qiT_ref[(g * pair + p) * IDX_DIM:(g * pair + p + 1) * IDX_DIM, :]
                               for p in range(pair)], axis=1) for g in range(IDX_HEADS // pair)]

    def score_body(c, amax):
        for r in range(ch // cha):
            start = pl.multiple_of(c * ch + r * cha, cha)
            kic = ki_ref[pl.ds(start, cha), :]
            acc = jnp.zeros((cha, tq), F32)
            for g in range(IDX_HEADS // pair):
                x = _dot(kic, qi_rhs[g])
                for p in range(pair):
                    h = g * pair + p
                    acc = acc + w_idx[h:h + 1, :] * jnp.maximum(x[:, p * tq:(p + 1) * tq], 0.0)
            amax = jnp.maximum(amax, _fold_rows(jnp.abs(acc), jnp.maximum))
            sc_ref[pl.ds(start, cha), :] = jnp.where(start + row_iota[0:cha] <= q_pos, acc, -jnp.inf)
        return amax

    amax = lax.fori_loop(0, n_ch, score_body, jnp.zeros((SUBLANES, tq), F32))
    amax = jnp.max(amax, axis=0, keepdims=True)

    def key_reduce(fn, init, combine, final):
        def body(c, acc):
            start = pl.multiple_of(c * ch, ch)
            return combine(acc, _fold_rows(fn(sc_ref[pl.ds(start, ch), :], start + row_iota), combine))
        return final(lax.fori_loop(0, n_ch, body, jnp.full((SUBLANES, tq), init, F32)), axis=0, keepdims=True)

    count = lambda fn: key_reduce(fn, 0.0, jnp.add, jnp.sum)
    lowest = lambda fn: key_reduce(fn, jnp.inf, jnp.minimum, jnp.min)
    count_ge = lambda t: count(lambda x, _: jnp.where(x >= t, 1.0, 0.0))
    min_ge = lambda t: lowest(lambda x, _: jnp.where(x >= t, x, jnp.inf))
    def count_gt_min_gt(t):
        def body(c, acc):
            x = sc_ref[pl.ds(pl.multiple_of(c * ch, ch), ch), :]
            gt = x > t
            return (acc[0] + _fold_rows(jnp.where(gt, 1.0, 0.0), jnp.add),
                    jnp.minimum(acc[1], _fold_rows(jnp.where(gt, x, jnp.inf), jnp.minimum)))
        init = (jnp.zeros((SUBLANES, tq), F32), jnp.full((SUBLANES, tq), jnp.inf, F32))
        cnt, low = lax.fori_loop(0, n_ch, body, init)
        return jnp.sum(cnt, axis=0, keepdims=True), jnp.min(low, axis=0, keepdims=True)
    count_tie_le = lambda t, jm: count(
        lambda x, kp: jnp.where(jnp.logical_and(x == t, kp <= jm), 1.0, 0.0))

    n_valid = (q_pos + 1).astype(F32)
    need = jnp.minimum(n_valid, float(min(TOP_K_MAX, seq // 4)))
    t_val, j_val = _select_topk(count_ge, min_ge, count_gt_min_gt, count_tie_le, amax, need, n_valid, seq)

    m_ref[...] = jnp.full(m_ref.shape, NEG_BIG, F32)
    l_ref[...] = jnp.zeros(l_ref.shape, F32)
    acc_ref[...] = jnp.zeros(acc_ref.shape, F32)
    zero_q = jnp.zeros((HEAD_DIM, tq), BF16)
    q_rhs = []
    for g in range(N_HEADS // pair):
        blocks = []
        for p in range(pair):
            qh = qT_ref[(g * pair + p) * HEAD_DIM:(g * pair + p + 1) * HEAD_DIM, :]
            blocks.append(jnp.concatenate([qh if pp == p else zero_q for pp in range(pair)], axis=1))
        q_rhs.append(jnp.concatenate(blocks, axis=0))
    ones_rows = jnp.ones((BF16_ROWS, cat), BF16)

    n_steps = n_ch * (ch // cat)

    def logits_pass(c, slot):
        start = pl.multiple_of(c * cat, cat)
        x = sc_ref[pl.ds(start, cat), :]
        sel = jnp.logical_or(x > t_val, jnp.logical_and(x == t_val, start + row_iota[0:cat] <= j_val))
        bias = jnp.where(sel, 0.0, -jnp.inf)
        m_cur = []
        for g in range(N_HEADS // pair):
            kc = k_ref[pl.ds(start, cat), g * pair * HEAD_DIM:(g + 1) * pair * HEAD_DIM]
            s_all = _dot(kc, q_rhs[g])
            for p in range(pair):
                s = s_all[:, p * tq:(p + 1) * tq] + bias
                s_ref[slot, g * pair + p] = s
                m_cur.append(jnp.max(_fold_rows(s, jnp.maximum), axis=0, keepdims=True))
        return jnp.concatenate(m_cur, axis=0)

    def softmax_pass(c, slot, m_cur):
        start = pl.multiple_of(c * cat, cat)
        m_prev = m_ref[...]
        m_new = jnp.maximum(m_prev, m_cur)
        alpha = jnp.exp2(m_prev - m_new)
        m_ref[...] = m_new
        l_cur = []
        for h in range(N_HEADS):
            pr = jnp.exp2(s_ref[slot, h] - m_new[h:h + 1, :]).astype(BF16)
            lhs = jnp.concatenate([vT_ref[h * HEAD_DIM:(h + 1) * HEAD_DIM, pl.ds(start, cat)], ones_rows], axis=0)
            pv = _dot(lhs, pr)
            rows = pl.ds(h * HEAD_DIM, HEAD_DIM)
            acc_ref[rows, :] = alpha[h:h + 1, :] * acc_ref[rows, :] + pv[0:HEAD_DIM]
            l_cur.append(pv[HEAD_DIM:HEAD_DIM + 1])
        l_ref[...] = alpha * l_ref[...] + jnp.concatenate(l_cur, axis=0)

    def attn_body(cc, m_cur):
        c0 = 2 * cc
        m_1 = logits_pass(c0 + 1, 1)
        softmax_pass(c0, 0, m_cur)
        m_2 = logits_pass(jnp.minimum(c0 + 2, n_steps - 1), 0)
        softmax_pass(c0 + 1, 1, m_1)
        return m_2

    lax.fori_loop(0, n_steps // 2, attn_body, logits_pass(0, 0))
    outs =[acc_ref[h * HEAD_DIM:(h + 1) * HEAD_DIM, :] / l_ref[h:h + 1, :] for h in range(N_HEADS)]
    o_ref[...] = jnp.concatenate(outs, axis=0).T


def _prompt_attention(qT, qiT, wiT, k, vT, ki, tq, ch):
    seq = qT.shape[1]
    cha = min(ch, MXU_WIDTH)
    cat = ch // 2
    kern = functools.partial(_prompt_attn_kernel, tq=tq, ch=ch, cha=cha, cat=cat, seq=seq)
    col = lambda r: pl.BlockSpec((r, tq), lambda i: (0, i))
    return pl.pallas_call(
        kern, grid=(seq // tq,),
        in_specs=[col(ATT_WIDTH), col(IDX_WIDTH), col(IDX_HEADS),
                  _const_spec(k.shape), _const_spec(vT.shape), _const_spec(ki.shape)],
        out_specs=pl.BlockSpec((tq, ATT_WIDTH), lambda i: (i, 0)),
        out_shape=jax.ShapeDtypeStruct((seq, ATT_WIDTH), F32),
        scratch_shapes=[pltpu.VMEM((seq, tq), F32),
                        pltpu.VMEM((N_HEADS, tq), F32),
                        pltpu.VMEM((N_HEADS, tq), F32),
                        pltpu.VMEM((ATT_WIDTH, tq), F32),
                        pltpu.VMEM((2, N_HEADS, cat, tq), F32)],
        compiler_params=_params(("arbitrary",)), name="prompt_attention",
    )(qT, qiT, wiT, k, vT, ki)


def _lru_gates(xc, wrg_ref, brg_ref, wig_ref, big_ref, lam_ref):
    xcb = xc.astype(BF16)
    r = _sigmoid(_dot(xcb, wrg_ref[...]) + brg_ref[...])
    ig = _sigmoid(_dot(xcb, wig_ref[...]) + big_ref[...])
    nlam = -lam_ref[...]
    softplus = jnp.maximum(nlam, 0.0) + _log1p(jnp.exp(-jnp.abs(nlam)))
    log_a = (-LRU_C) * r * softplus
    a = jnp.exp(log_a)
    u = jnp.sqrt(-_expm1(2.0 * log_a)) * (ig * xc)
    return a, u


def _merge(x, ya, g_att, yl, g_lru, ga, gb, woa_ref, wob_ref, wout_ref):
    pa = _dot((ya * _silu(g_att)).astype(BF16), woa_ref[...])
    pb = _dot((yl * _silu(g_lru)).astype(BF16), wob_ref[...])
    m = _sigmoid(ga) * pa + _sigmoid(gb) * pb
    return x + _dot(m.astype(BF16), wout_ref[...])


def _prompt_lru_kernel(x_ref, g_ref, wlru_ref, cw_ref, cb_ref, wrg_ref, brg_ref, wig_ref, big_ref, lam_ref,
                       ya_ref, gatt_ref, woa_ref, wob_ref, wout_ref,
                       xo_ref, tail_ref, hlast_ref, xl_sc, h_sc, *, tm, width):
    i = pl.program_id(0)

    @pl.when(i == 0)
    def _():
        xl_sc[0:SUBLANES, :] = jnp.zeros((SUBLANES, width), F32)
        h_sc[...] = jnp.zeros(h_sc.shape, F32)

    x = x_ref[...]
    hb = _rmsnorm(x, g_ref[...]).astype(BF16)
    z = _dot(hb, wlru_ref[...])
    xl_sc[SUBLANES:SUBLANES + tm, :] = z[:, 0:width]
    xc = cb_ref[...]
    for j in range(CONV_WIDTH):
        xc = xc + cw_ref[j:j + 1, :] * xl_sc[pl.ds(SUBLANES - (CONV_WIDTH - 1) + j, tm), :]
    a, b = _lru_gates(xc, wrg_ref, brg_ref, wig_ref, big_ref, lam_ref)

    t_idx = lax.broadcasted_iota(I32, (tm, width), 0)
    d = 1
    while d < tm:
        a_s = pltpu.roll(a, d, 0)
        b_s = pltpu.roll(b, d, 0)
        live = t_idx >= d
        b = jnp.where(live, a * b_s + b, b)
        a = jnp.where(live, a * a_s, a)
        d *= 2
    hs = a * h_sc[0:1, :] + b
    h_sc[...] = jnp.broadcast_to(hs[tm - 1:tm, :], h_sc.shape)
    hlast_ref[...] = h_sc[...]
    tail_ref[...] = xl_sc[tm:tm + SUBLANES, :]
    xl_sc[0:SUBLANES, :] = xl_sc[tm:tm + SUBLANES, :]

    xo_ref[...] = _merge(x, ya_ref[...], gatt_ref[...], hs, z[:, width:2 * width],
                         z[:, 2 * width:3 * width], z[:, 3 * width:4 * width], woa_ref, wob_ref, wout_ref)


def _prompt_lru_merge(x, ya, gatt, w, tm):
    m, d = x.shape
    width = w["rg"].shape[0]
    kern = functools.partial(_prompt_lru_kernel, tm=tm, width=width)
    row = lambda c: pl.BlockSpec((tm, c), lambda i: (i, 0))
    consts = [w["g"], w["lru"], w["conv_w"], w["conv_b"], w["rg"], w["b_rg"], w["ig"], w["b_ig"], w["lam"]]
    consts2 = [w["oa"], w["ob"], w["out"]]
    state = pl.BlockSpec((SUBLANES, width), lambda i: (0, 0))
    return pl.pallas_call(
        kern, grid=(m // tm,),
        in_specs=[row(d)] + [_const_spec(c.shape) for c in consts] + [row(ATT_WIDTH), row(ATT_WIDTH)]
                 + [_const_spec(c.shape) for c in consts2],
        out_specs=(row(d), state, state),
        out_shape=(jax.ShapeDtypeStruct((m, d), F32), jax.ShapeDtypeStruct((SUBLANES, width), F32),
                   jax.ShapeDtypeStruct((SUBLANES, width), F32)),
        scratch_shapes=[pltpu.VMEM((tm + SUBLANES, width), F32), pltpu.VMEM((SUBLANES, width), F32)],
        compiler_params=_params(("arbitrary",)), name="prompt_lru_merge",
    )(x, *consts, ya, gatt, *consts2)


def _sample_lru_kernel(x_ref, g_ref, wlru_ref, cw_ref, cb_ref, wrg_ref, brg_ref, wig_ref, big_ref, lam_ref,
                       c0_ref, c1_ref, c2_ref, h0_ref, ya_ref, gatt_ref, woa_ref, wob_ref, wout_ref,
                       xo_ref, xl_ref, h_ref, *, width):
    x = x_ref[...]
    hb = _rmsnorm(x, g_ref[...]).astype(BF16)
    z = _dot(hb, wlru_ref[...])
    xl = z[:, 0:width]
    xc = cb_ref[...]
    for j, past in enumerate((c0_ref, c1_ref, c2_ref)):
        xc = xc + cw_ref[j:j + 1, :] * past[...]
    xc = xc + cw_ref[CONV_WIDTH - 1:CONV_WIDTH, :] * xl
    a, u = _lru_gates(xc, wrg_ref, brg_ref, wig_ref, big_ref, lam_ref)
    h = u + a * h0_ref[...]
    xl_ref[...] = xl
    h_ref[...] = h
    xo_ref[...] = _merge(x, ya_ref[...], gatt_ref[...], h, z[:, width:2 * width],
                         z[:, 2 * width:3 * width], z[:, 3 * width:4 * width], woa_ref, wob_ref, wout_ref)


def _sample_lru_merge(x, conv_state, h0, ya, gatt, w):
    b, d = x.shape
    width = w["rg"].shape[0]
    kern = functools.partial(_sample_lru_kernel, width=width)
    args = [x, w["g"], w["lru"], w["conv_w"], w["conv_b"], w["rg"], w["b_rg"], w["ig"], w["b_ig"], w["lam"],
            conv_state[:, 0], conv_state[:, 1], conv_state[:, 2], h0, ya, gatt, w["oa"], w["ob"], w["out"]]
    full = lambda a: pl.BlockSpec(a.shape, lambda i: (0,) * a.ndim)
    out_shape = (jax.ShapeDtypeStruct((b, d), F32), jax.ShapeDtypeStruct((b, width), F32),
                 jax.ShapeDtypeStruct((b, width), F32))
    return pl.pallas_call(
        kern, grid=(1,), in_specs=[full(a) for a in args],
        out_specs=tuple(pl.BlockSpec(s.shape, lambda i: (0, 0)) for s in out_shape), out_shape=out_shape,
        compiler_params=_params(("arbitrary",)), name="sample_lru_merge",
    )(*args)


def _sample_score_kernel(pt_ref, qi_ref, wi_ref, kin_ref, cki_ref, o_ref, kbuf, sem,
                         *, layer_page0, n_pages, page, ch):
    b = pl.program_id(0)

    def page_copy(p):
        phys = pt_ref[b, p]
        return pltpu.make_async_copy(cki_ref.at[layer_page0 + phys],
                                     kbuf.at[:, pl.ds(pl.multiple_of(p * page, page), page)], sem)

    def start(p, c):
        page_copy(p).start()
        return c

    def wait(p, c):
        page_copy(p).wait()
        return c

    lax.fori_loop(0, n_pages, start, 0)
    lax.fori_loop(0, n_pages, wait, 0)

    qi = qi_ref[0]
    w = wi_ref[0] * IDX_SCALE
    n_sub = ch // LANES

    def body(c, carry):
        start_col = pl.multiple_of(c * ch, ch)
        x = _dot(qi, kbuf[:, pl.ds(start_col, ch)].astype(BF16))
        s = jnp.sum(w * jnp.maximum(x, 0.0), axis=0, keepdims=True)
        for j in range(n_sub):
            o_ref[0, pl.ds(c * n_sub + j, 1), :] = s[:, j * LANES:(j + 1) * LANES]
        return carry

    lax.fori_loop(0, (n_pages * page) // ch, body, 0)

    kn = kin_ref[0].astype(BF16).astype(F32)
    xn = jnp.sum(qi.astype(F32) * kn, axis=-1, keepdims=True)
    sn = jnp.sum(w * jnp.maximum(xn, 0.0), axis=0, keepdims=True)
    n_rows = (n_pages * page) // LANES
    tail_rows = o_ref.shape[1] - n_rows
    r_i = lax.broadcasted_iota(I32, (tail_rows, LANES), 0)
    l_i = lax.broadcasted_iota(I32, (tail_rows, LANES), 1)
    o_ref[0, n_rows:n_rows + tail_rows, :] = jnp.where(jnp.logical_and(r_i == 0, l_i == 0), sn, -jnp.inf)


def _sample_scores(page_table, qi, wi, ki_new, ckiT, layer, n_pool, ch):
    b, n_pages = page_table.shape
    page = ckiT.shape[2]
    n_rows = pl.cdiv((n_pages * page) // LANES + 1, SUBLANES) * SUBLANES
    kern = functools.partial(_sample_score_kernel, layer_page0=layer * n_pool, n_pages=n_pages, page=page, ch=ch)
    grid_spec = pltpu.PrefetchScalarGridSpec(
        num_scalar_prefetch=1, grid=(b,),
        in_specs=[pl.BlockSpec((1, IDX_HEADS, IDX_DIM), lambda i, pt: (i, 0, 0)),
                  pl.BlockSpec((1, IDX_HEADS, 1), lambda i, pt: (i, 0, 0)),
                  pl.BlockSpec((1, 1, IDX_DIM), lambda i, pt: (i, 0, 0)),
                  pl.BlockSpec(memory_space=pl.ANY)],
        out_specs=pl.BlockSpec((1, n_rows, LANES), lambda i, pt: (i, 0, 0)),
        scratch_shapes=[pltpu.VMEM((IDX_DIM, n_pages * page), F32), pltpu.SemaphoreType.DMA(())],
    )
    return pl.pallas_call(
        kern, grid_spec=grid_spec, out_shape=jax.ShapeDtypeStruct((b, n_rows, LANES), F32),
        compiler_params=_params(("arbitrary",)), name="sample_scores",
    )(page_table, qi, wi, ki_new, ckiT)


def _sample_select_kernel(sc_ref, bias_ref, *, n_keys, k_top):
    x = sc_ref[...]
    nb = x.shape[0]
    kp = (lax.broadcasted_iota(I32, x.shape, 1) * LANES + lax.broadcasted_iota(I32, x.shape, 2))
    red = lambda a, op: op(op(a, axis=2, keepdims=True), axis=1, keepdims=True)
    amax = red(jnp.where(x == -jnp.inf, 0.0, jnp.abs(x)), jnp.max)
    count_ge = lambda t: red(jnp.where(x >= t, 1.0, 0.0), jnp.sum)
    min_ge = lambda t: red(jnp.where(x >= t, x, jnp.inf), jnp.min)
    count_gt_min_gt = lambda t: (red(jnp.where(x > t, 1.0, 0.0), jnp.sum),
                                 red(jnp.where(x > t, x, jnp.inf), jnp.min))
    count_tie_le = lambda t, jm: red(jnp.where(jnp.logical_and(x == t, kp <= jm), 1.0, 0.0), jnp.sum)
    need = jnp.full((nb, 1, 1), float(k_top), F32)
    n_valid = jnp.full((nb, 1, 1), float(n_keys), F32)
    t_val, j_val = _select_topk(count_ge, min_ge, count_gt_min_gt, count_tie_le, amax, need, n_valid, n_keys)
    sel = jnp.logical_or(x > t_val, jnp.logical_and(x == t_val, kp <= j_val))
    bias_ref[...] = jnp.where(sel, 0.0, -jnp.inf)


def _sample_select(scores, n_keys, k_top):
    kern = functools.partial(_sample_select_kernel, n_keys=n_keys, k_top=k_top)
    full = pl.BlockSpec(scores.shape, lambda i: (0, 0, 0))
    return pl.pallas_call(
        kern, grid=(1,), in_specs=[full], out_specs=full,
        out_shape=jax.ShapeDtypeStruct(scores.shape, F32),
        compiler_params=_params(("arbitrary",)), name="sample_select",
    )(scores)


def _sample_attn_kernel(pt_ref, bias_ref, q_ref, kn_ref, vn_ref, ck_ref, cv_ref, o_ref,
                        kbuf, vbuf, acc_sc, sem, *, layer_page0, n_pages, page, group):
    b = pl.program_id(0)
    n_groups = n_pages // group

    def copies(g, slot):
        out = []
        for j in range(group):
            phys = pt_ref[b, g * group + j]
            out.append(pltpu.make_async_copy(ck_ref.at[layer_page0 + phys], kbuf.at[slot, j], sem.at[0, slot]))
            out.append(pltpu.make_async_copy(cv_ref.at[layer_page0 + phys], vbuf.at[slot, j], sem.at[1, slot]))
        return out

    def start(g, slot):
        for cp in copies(g, slot):
            cp.start()

    def wait(g, slot):
        for cp in copies(g, slot):
            cp.wait()

    q = q_ref[0]
    qb = jnp.broadcast_to(q, (N_HEADS, HEAD_DIM, page))
    acc_sc[...] = jnp.zeros(acc_sc.shape, F32)
    start(0, 0)

    def body(g, carry):
        m_prev, l_prev = carry
        slot = g % 2

        @pl.when(g + 1 < n_groups)
        def _():
            start(g + 1, 1 - slot)

        wait(g, slot)

        s = []
        for j in range(group):
            kt = kbuf[slot, j]
            sj = jnp.sum(kt * qb, axis=1, keepdims=True)
            s.append(sj + bias_ref[0, pl.ds(g * group + j, 1), :][None])
        m_cur = s[0]
        for sj in s[1:]:
            m_cur = jnp.maximum(m_cur, sj)
        m_new = jnp.maximum(m_prev, jnp.max(m_cur, axis=2, keepdims=True))
        alpha = jnp.exp2(m_prev - m_new)
        pv = jnp.zeros((N_HEADS, HEAD_DIM, page), F32)
        l_cur = jnp.zeros((N_HEADS, 1, page), F32)
        for j in range(group):
            pj = jnp.exp2(s[j] - m_new)
            l_cur = l_cur + pj
            pv = pv + vbuf[slot, j] * pj
        acc_sc[...] = alpha * acc_sc[...] + pv
        return m_new, alpha * l_prev + jnp.sum(l_cur, axis=2, keepdims=True)

    init = (jnp.full((N_HEADS, 1, 1), NEG_BIG, F32), jnp.zeros((N_HEADS, 1, 1), F32))
    m_run, l_run = lax.fori_loop(0, n_groups, body, init)

    s_new = (jnp.sum(kn_ref[0] * q, axis=1, keepdims=True)
             + bias_ref[0, n_pages:n_pages + 1, 0:1][None])
    m_fin = jnp.maximum(m_run, s_new)
    alpha = jnp.exp2(m_run - m_fin)
    p_new = jnp.exp2(s_new - m_fin)
    l_fin = alpha * l_run + p_new
    out = (alpha * jnp.sum(acc_sc[...], axis=2, keepdims=True) + p_new * vn_ref[0]) / l_fin
    o_ref[0] = out


def _sample_attention(bias, page_table, q, k_new, v_new, ckT, cvT, layer, n_pool, group):
    b, n_pages = page_table.shape
    page = ckT.shape[-1]
    kern = functools.partial(_sample_attn_kernel, layer_page0=layer * n_pool, n_pages=n_pages, page=page,
                             group=group)
    per_b = pl.BlockSpec((1, N_HEADS, HEAD_DIM, 1), lambda i, p: (i, 0, 0, 0))
    grid_spec = pltpu.PrefetchScalarGridSpec(
        num_scalar_prefetch=1, grid=(b,),
        in_specs=[pl.BlockSpec((1,) + bias.shape[1:], lambda i, p: (i, 0, 0)),
                  per_b, per_b, per_b,
                  pl.BlockSpec(memory_space=pl.ANY), pl.BlockSpec(memory_space=pl.ANY)],
        out_specs=per_b,
        scratch_shapes=[pltpu.VMEM((2, group, N_HEADS, HEAD_DIM, page), F32),
                        pltpu.VMEM((2, group, N_HEADS, HEAD_DIM, page), F32),
                        pltpu.VMEM((N_HEADS, HEAD_DIM, page), F32),
                        pltpu.SemaphoreType.DMA((2, 2))],
    )
    col = lambda a: a.reshape(b, N_HEADS, HEAD_DIM, 1)
    out = pl.pallas_call(
        kern, grid_spec=grid_spec, out_shape=jax.ShapeDtypeStruct((b, N_HEADS, HEAD_DIM, 1), F32),
        compiler_params=_params(("arbitrary",)), name="sample_attention",
    )(page_table, bias, col(q), col(k_new), col(v_new), ckT, cvT)
    return out.reshape(b, ATT_WIDTH)


def _norm_kernel(x_ref, g_ref, o_ref):
    o_ref[...] = _rmsnorm(x_ref[...], g_ref[...])


def _final_norm(x, g, tm):
    m, d = x.shape
    return pl.pallas_call(
        _norm_kernel, grid=(m // tm,),
        in_specs=[pl.BlockSpec((tm, d), lambda i: (i, 0)), pl.BlockSpec((1, d), lambda i: (0, 0))],
        out_specs=pl.BlockSpec((tm, d), lambda i: (i, 0)), out_shape=jax.ShapeDtypeStruct((m, d), F32),
        compiler_params=_params(("arbitrary",)), name="final_norm",
    )(x, g)


def _layer_weights(l, norm_g, w_in, conv_w, conv_b, w_rg, b_rg, w_ig, b_ig, lru_lam, w_oa, w_ob, w_out):
    d = w_in.shape[1]
    width = lru_lam.shape[1]
    wl = w_in[l]
    o_g = 3 * ATT_WIDTH
    o_qi = 4 * ATT_WIDTH
    o_ki = o_qi + IDX_WIDTH
    o_wi = o_ki + IDX_DIM
    o_lru = o_wi + IDX_HEADS
    att_t = jnp.concatenate([wl[:, 0:o_g], wl[:, o_qi:o_lru]], axis=1).T
    att_t = jnp.pad(att_t, ((0, (-att_t.shape[0]) % BF16_ROWS), (0, 0)))
    blockdiag = lambda w: jax.scipy.linalg.block_diag(*[w[n] for n in range(w.shape[0])])
    r2 = lambda v: v.reshape(1, -1)
    return {
        "g": r2(norm_g[l]),
        "attT": att_t.astype(BF16),
        "k": wl[:, ATT_WIDTH:2 * ATT_WIDTH].astype(BF16),
        "gatt": wl[:, o_g:o_qi].astype(BF16),
        "ki": wl[:, o_ki:o_wi].astype(BF16),
        "lru": wl[:, o_lru:o_lru + 2 * width + 2 * d].astype(BF16),
        "conv_w": conv_w[l], "conv_b": r2(conv_b[l]),
        "rg": blockdiag(w_rg[l]).astype(BF16), "b_rg": r2(b_rg[l]),
        "ig": blockdiag(w_ig[l]).astype(BF16), "b_ig": r2(b_ig[l]),
        "lam": r2(lru_lam[l]),
        "oa": w_oa[l].astype(BF16), "ob": w_ob[l].astype(BF16), "out": w_out[l].astype(BF16),
    }


def _pick_tile(m, target):
    t = min(m, target)
    while m % t:
        t //= 2
    return t


def kernel(x_prompt, x_sample, cache_k, cache_v, cache_kidx, state_conv, state_h, page_table,
           norm_g, w_in, conv_w, conv_b, w_rg, b_rg, w_ig, b_ig, lru_lam, w_oa, w_ob, w_out, final_g):
    bp, seq, d = x_prompt.shape
    bs, dec_seq, _ = x_sample.shape
    depth, n_pool, page = cache_k.shape[0], cache_k.shape[1], cache_k.shape[2]
    assert bp == 1 and dec_seq == 1, "one prompt sequence and one new token per sampled sequence"
    assert page == LANES, "a cache page fills the lane axis"
    n_pages = page_table.shape[1]
    past = n_pages * page
    k_top_s = min(TOP_K_MAX, (past + dec_seq) // 4)

    ckT = jnp.transpose(cache_k, (0, 1, 3, 4, 2)).reshape(depth * n_pool, N_HEADS, HEAD_DIM, page)
    cvT = jnp.transpose(cache_v, (0, 1, 3, 4, 2)).reshape(depth * n_pool, N_HEADS, HEAD_DIM, page)
    ckiT = jnp.transpose(cache_kidx, (0, 1, 3, 2)).reshape(depth * n_pool, IDX_DIM, page)

    tm_proj = _pick_tile(seq, 256)
    tq = _pick_tile(seq, 128)
    ch = _pick_tile(seq, 512)
    ch_s = _pick_tile(past, 1024)
    group = _pick_tile(n_pages, 8)

    xp = x_prompt.reshape(seq, d)
    xs = x_sample.reshape(bs, d)
    outs = {n: [] for n in ("pk", "pv", "pki", "pconv", "ph", "sk", "sv", "ski", "sconv", "sh")}
    heads_last = lambda t, n: jnp.transpose(t.reshape(N_HEADS, HEAD_DIM, n), (2, 0, 1))
    for l in range(depth):
        w = _layer_weights(l, norm_g, w_in, conv_w, conv_b, w_rg, b_rg, w_ig, b_ig, lru_lam, w_oa, w_ob, w_out)
        qT, kT, vT, qiT, kiT, wiT, _, vTb, kb, kib, gatt = _proj(xp, w["g"], w, tm_proj)
        ya = _prompt_attention(qT, qiT, wiT, kb, vTb, kib, tq, ch)
        xp, tail, hlast = _prompt_lru_merge(xp, ya, gatt, w, tm_proj)
        outs["pk"].append(heads_last(kT, seq)[None])
        outs["pv"].append(heads_last(vT, seq)[None])
        outs["pki"].append(kiT.T[None])
        outs["pconv"].append(tail[SUBLANES - (CONV_WIDTH - 1):].reshape(1, CONV_WIDTH - 1, -1))
        outs["ph"].append(hlast[0:1])
        qT, kT, vT, qiT, kiT, wiT, _, _, _, _, gatt = _proj(xs, w["g"], w, bs)
        k_new, v_new, ki_new = kT.T, vT.T, kiT.T
        scores = _sample_scores(page_table, qiT.T.reshape(bs, IDX_HEADS, IDX_DIM), wiT.T.reshape(bs, IDX_HEADS, 1),
                                ki_new.reshape(bs, 1, IDX_DIM), ckiT, l, n_pool, ch_s)
        bias = _sample_select(scores, past + dec_seq, k_top_s)
        ya = _sample_attention(bias, page_table, qT.T.astype(F32), k_new, v_new, ckT, cvT, l, n_pool, group)
        xs, xl, h = _sample_lru_merge(xs, state_conv[l], state_h[l], ya, gatt, w)
        outs["sk"].append(k_new.reshape(bs, 1, N_HEADS, HEAD_DIM))
        outs["sv"].append(v_new.reshape(bs, 1, N_HEADS, HEAD_DIM))
        outs["ski"].append(ki_new.reshape(bs, 1, IDX_DIM))
        outs["sconv"].append(jnp.concatenate([state_conv[l][:, 1:], xl[:, None, :]], axis=1))
        outs["sh"].append(h)
    y_prompt = _final_norm(xp, final_g.reshape(1, d), tm_proj).reshape(1, seq, d)
    y_sample = _final_norm(xs, final_g.reshape(1, d), bs).reshape(bs, 1, d)
    st = lambda n: jnp.stack(outs[n])
    return (y_prompt, y_sample, st("pk"), st("pv"), st("pki"), st("pconv"), st("ph"),
            st("sk"), st("sv"), st("ski"), st("sconv"), st("sh"))
```
